```python
import jax, jax.numpy as jnp
from jax import lax
import numpy as np

D_MODEL = 2048
BATCH = 8
SEQ = 2048
DEPTH = 1
DEC_BATCH = 128
DEC_SEQ = 4
PAST_LEN = 2048
PAGE_SIZE = 128

N_HEADS = 16
HEAD_DIM = 64
D_ATTN = N_HEADS * HEAD_DIM
D_CONV = 1024
CONV_W = 3
D_FF = ((8 * D_MODEL // 3 + 255) // 256) * 256
Q_BLOCK = 128
RMS_EPS = 1e-6
ATTN_SCALE = HEAD_DIM ** -0.5
FORGET_BIAS_INIT = 4.0
IN_SIZES = (D_ATTN, D_ATTN, D_ATTN, N_HEADS, D_CONV, D_CONV, D_CONV, D_MODEL, D_MODEL)
N_IN = sum(IN_SIZES)

kernel_name = "fox_shortconv_gated_hybrid_step"


def _rmsnorm(x, g):
    xf = x.astype(jnp.float32)
    y = xf * lax.rsqrt(jnp.mean(xf * xf, axis=-1, keepdims=True) + RMS_EPS)
    return (y * g.astype(jnp.float32)).astype(x.dtype)


def _split_cols(z, sizes):
    offs = np.cumsum(sizes)[:-1].tolist()
    return jnp.split(z, offs, axis=-1)


def _adaln(c, w_ada, b_ada):
    m = jax.nn.silu(c) @ w_ada + b_ada
    return jnp.split(m[:, None, :], 6, axis=-1)


def _mixer_inputs(x, shift, scale, g_pre, w_in, b_f):
    B, T, _ = x.shape
    h = _rmsnorm(x, g_pre) * (1.0 + scale) + shift
    q, k, v, f, u, bg, cg, ga, gb = _split_cols(h @ w_in, IN_SIZES)
    logf = jax.nn.log_sigmoid((f + b_f).astype(jnp.float32))
    heads = lambda t: t.reshape(B, T, N_HEADS, HEAD_DIM)
    return heads(q), heads(k), heads(v), logf, cg * u, bg, ga, gb


def _fox_softmax(s, cum_q, cum_k, mask):
    bias = jnp.swapaxes(cum_q, 1, 2)[..., :, None] - jnp.swapaxes(cum_k, 1, 2)[..., None, :]
    logits = s.astype(jnp.float32) * ATTN_SCALE + bias.astype(jnp.float32)
    logits = jnp.where(mask, logits, -jnp.inf)
    return jax.nn.softmax(logits, axis=-1)


def _prompt_attention(q, k, v, cum):
    B, T, H, Dh = q.shape
    nb = T // Q_BLOCK
    qb = q.reshape(B, nb, Q_BLOCK, H, Dh).swapaxes(0, 1)
    cb = cum.reshape(B, nb, Q_BLOCK, H).swapaxes(0, 1)
    kpos = jnp.arange(T, dtype=jnp.int32)
    pb = kpos.reshape(nb, Q_BLOCK)

    def block(args):
        q_i, c_i, p_i = args
        s = jnp.einsum('bqhd,bkhd->bhqk', q_i, k)
        p = _fox_softmax(s, c_i, cum, kpos[None, :] <= p_i[:, None])
        return jnp.einsum('bhqk,bkhd->bqhd', p.astype(v.dtype), v)

    o = lax.map(block, (qb, cb, pb))
    return o.swapaxes(0, 1).reshape(B, T, H, Dh)


def _sample_attention(q, k_new, v_new, cum_q, cum_all, k_past, v_past):
    P = k_past.shape[1]
    T = q.shape[1]
    s = jnp.concatenate([jnp.einsum('bqhd,bkhd->bhqk', q, k_past),
                         jnp.einsum('bqhd,bkhd->bhqk', q, k_new)], axis=-1)
    qpos = P + jnp.arange(T, dtype=jnp.int32)
    kpos = jnp.arange(P + T, dtype=jnp.int32)
    p = _fox_softmax(s, cum_q, cum_all, kpos[None, :] <= qpos[:, None]).astype(v_new.dtype)
    return (jnp.einsum('bhqk,bkhd->bqhd', p[..., :P], v_past)
            + jnp.einsum('bhqk,bkhd->bqhd', p[..., P:], v_new))


def _conv_causal(zp, w_conv, T):
    y = zp[:, 0:T] * w_conv[0]
    for j in range(1, CONV_W):
        y = y + zp[:, j:j + T] * w_conv[j]
    return y


def _finish(x, o, y_conv, bg, ga, gb, gate1, shift2, scale2, gate2,
            w_oa, w_ob, w_o, g_post1, g_pre2, w_ffn_in, w_ffn_out, g_post2):
    B, T, _ = x.shape
    a = o.reshape(B, T, D_ATTN) @ w_oa
    b = (bg * y_conv) @ w_ob
    m = (jax.nn.sigmoid(ga) * a + jax.nn.sigmoid(gb) * b) @ w_o
    x = x + gate1 * _rmsnorm(m, g_post1)
    h = _rmsnorm(x, g_pre2) * (1.0 + scale2) + shift2
    g, u = jnp.split(h @ w_ffn_in, 2, axis=-1)
    f = (jax.nn.silu(g) * u) @ w_ffn_out
    return x + gate2 * _rmsnorm(f, g_post2)


def setup_inputs(seed: int = 0) -> dict:
    key = jax.random.key(seed)
    ks = jax.random.split(key, 24)
    n_pages = PAST_LEN // PAGE_SIZE
    n_pool = (DEC_BATCH * n_pages * 5) // 4
    nrm = lambda k, shape, s=1.0: jax.random.normal(k, shape, jnp.float32) * s
    page_table = jax.random.permutation(ks[0], n_pool)[:DEC_BATCH * n_pages]
    page_table = page_table.reshape(DEC_BATCH, n_pages).astype(jnp.int32)
    return {
        "x_prompt": nrm(ks[1], (BATCH, SEQ, D_MODEL)),
        "x_sample": nrm(ks[2], (DEC_BATCH, DEC_SEQ, D_MODEL)),
        "cache_k": nrm(ks[3], (DEPTH, n_pool, PAGE_SIZE, N_HEADS, HEAD_DIM)),
        "cache_v": nrm(ks[4], (DEPTH, n_pool, PAGE_SIZE, N_HEADS, HEAD_DIM)),
        "cache_logf": jax.nn.log_sigmoid(FORGET_BIAS_INIT + nrm(ks[5], (DEPTH, n_pool, PAGE_SIZE, N_HEADS), 0.5)),
        "state_conv": nrm(ks[6], (DEPTH, DEC_BATCH, CONV_W - 1, D_CONV)),
        "page_table": page_table,
        "c_prompt": nrm(ks[7], (BATCH, D_MODEL)),
        "c_sample": nrm(ks[8], (DEC_BATCH, D_MODEL)),
        "w_ada": nrm(ks[9], (DEPTH, D_MODEL, 6 * D_MODEL), 0.5 * D_MODEL ** -0.5),
        "b_ada": nrm(ks[10], (DEPTH, 6 * D_MODEL), 0.02),
        "g_pre1": 1.0 + nrm(ks[11], (DEPTH, D_MODEL), 0.02),
        "w_in": nrm(ks[12], (DEPTH, D_MODEL, N_IN), D_MODEL ** -0.5),
        "b_f": FORGET_BIAS_INIT + nrm(ks[13], (DEPTH, N_HEADS), 0.5),
        "w_conv": nrm(ks[14], (DEPTH, CONV_W, D_CONV), CONV_W ** -0.5),
        "w_oa": nrm(ks[15], (DEPTH, D_ATTN, D_MODEL), D_ATTN ** -0.5),
        "w_ob": nrm(ks[16], (DEPTH, D_CONV, D_MODEL), D_CONV ** -0.5),
        "w_o": nrm(ks[17], (DEPTH, D_MODEL, D_MODEL), D_MODEL ** -0.5),
        "g_post1": 1.0 + nrm(ks[18], (DEPTH, D_MODEL), 0.02),
        "g_pre2": 1.0 + nrm(ks[19], (DEPTH, D_MODEL), 0.02),
        "w_ffn_in": nrm(ks[20], (DEPTH, D_MODEL, 2 * D_FF), D_MODEL ** -0.5),
        "w_ffn_out": nrm(ks[21], (DEPTH, D_FF, D_MODEL), D_FF ** -0.5),
        "g_post2": 1.0 + nrm(ks[22], (DEPTH, D_MODEL), 0.02),
    }


def reference(x_prompt, x_sample, cache_k, cache_v, cache_logf, state_conv, page_table,
              c_prompt, c_sample, w_ada, b_ada, g_pre1, w_in, b_f, w_conv, w_oa, w_ob, w_o,
              g_post1, g_pre2, w_ffn_in, w_ffn_out, g_post2):
    n_pages = page_table.shape[1]
    xp, xs = x_prompt, x_sample
    kp_l, vp_l, fp_l, cp_l, ks_l, vs_l, fs_l, cs_l = [], [], [], [], [], [], [], []
    for l in range(DEPTH):
        shared = (w_oa[l], w_ob[l], w_o[l], g_post1[l], g_pre2[l], w_ffn_in[l], w_ffn_out[l], g_post2[l])
        sh1, sc1, gt1, sh2, sc2, gt2 = _adaln(c_prompt, w_ada[l], b_ada[l])
        q, k, v, logf, zc, bg, ga, gb = _mixer_inputs(xp, sh1, sc1, g_pre1[l], w_in[l], b_f[l])
        cum = jnp.cumsum(logf, axis=1)
        o = _prompt_attention(q, k, v, cum)
        T = xp.shape[1]
        y_conv = _conv_causal(jnp.pad(zc, ((0, 0), (CONV_W - 1, 0), (0, 0))), w_conv[l], T)
        xp = _finish(xp, o, y_conv, bg, ga, gb, gt1, sh2, sc2, gt2, *shared)
        kp_l.append(k); vp_l.append(v); fp_l.append(logf); cp_l.append(zc[:, T - (CONV_W - 1):])
        sh1, sc1, gt1, sh2, sc2, gt2 = _adaln(c_sample, w_ada[l], b_ada[l])
        q, k, v, logf, zc, bg, ga, gb = _mixer_inputs(xs, sh1, sc1, g_pre1[l], w_in[l], b_f[l])
        Bd, Ts = xs.shape[:2]
        k_past = cache_k[l][page_table].reshape(Bd, n_pages * PAGE_SIZE, N_HEADS, HEAD_DIM)
        v_past = cache_v[l][page_table].reshape(Bd, n_pages * PAGE_SIZE, N_HEADS, HEAD_DIM)
        f_past = cache_logf[l][page_table].reshape(Bd, n_pages * PAGE_SIZE, N_HEADS)
        P = n_pages * PAGE_SIZE
        cum_all = jnp.cumsum(jnp.concatenate([f_past.astype(jnp.float32), logf], axis=1), axis=1)
        o = _sample_attention(q, k, v, cum_all[:, P:], cum_all, k_past, v_past)
        zp = jnp.concatenate([state_conv[l].astype(zc.dtype), zc], axis=1)
        y_conv = _conv_causal(zp, w_conv[l], Ts)
        xs = _finish(xs, o, y_conv, bg, ga, gb, gt1, sh2, sc2, gt2, *shared)
        ks_l.append(k); vs_l.append(v); fs_l.append(logf); cs_l.append(zp[:, Ts:])
    return (xp, xs,
            jnp.stack(kp_l), jnp.stack(vp_l), jnp.stack(fp_l), jnp.stack(cp_l),
            jnp.stack(ks_l), jnp.stack(vs_l), jnp.stack(fs_l), jnp.stack(cs_l))
```

```python
import functools

import jax
import jax.numpy as jnp
import numpy as np
from jax import lax
from jax.experimental import pallas as pl
from jax.experimental.pallas import tpu as pltpu

F32 = jnp.float32
BF16 = jnp.bfloat16

D_MODEL = 2048
N_HEADS = 16
HEAD_DIM = 64
D_ATTN = N_HEADS * HEAD_DIM
D_CONV = 1024
CONV_W = 3
D_FF = 5632
RMS_EPS = 1e-6
ATTN_SCALE = HEAD_DIM ** -0.5
PAGE_SIZE = 128

LANES = 128
N_HP = D_ATTN // LANES
N_CHUNK = 8
CHUNK_W = 6 * (D_ATTN // N_CHUNK) + 2 * (D_MODEL // N_CHUNK)
VMEM_LIMIT = 56 * 1024 * 1024


def _params(n_axes, vmem=VMEM_LIMIT):
    return pltpu.CompilerParams(
        dimension_semantics=("arbitrary",) * n_axes, vmem_limit_bytes=vmem)


def _rms(x, g):
    return x * lax.rsqrt(jnp.mean(x * x, axis=-1, keepdims=True) + RMS_EPS) * g


def _mod(ref):
    return ref[0] if len(ref.shape) == 3 else ref[...]


def _log_sigmoid(z):
    return jnp.minimum(z, 0.0) - jnp.log1p(jnp.exp(-jnp.abs(z)))


def _adaln_kernel(c_ref, w_ref, b_ref, o_ref):
    c = c_ref[...]
    s = (c * jax.nn.sigmoid(c)).astype(BF16)
    o_ref[...] = jnp.dot(s, w_ref[...].astype(BF16),
                         preferred_element_type=F32) + b_ref[...]


def _adaln(c, w_ada, b_ada):
    m, d = c.shape
    n = w_ada.shape[1]
    tn = 1024
    return pl.pallas_call(
        _adaln_kernel,
        grid=(n // tn,),
        in_specs=[pl.BlockSpec((m, d), lambda j: (0, 0)),
                  pl.BlockSpec((d, tn), lambda j: (0, j)),
                  pl.BlockSpec((1, tn), lambda j: (0, j))],
        out_specs=pl.BlockSpec((m, tn), lambda j: (0, j)),
        out_shape=jax.ShapeDtypeStruct((m, n), F32),
        compiler_params=_params(1),
        name="adaln",
    )(c, w_ada, b_ada.reshape(1, n))


def _in_proj_kernel(*refs, tm, blocks_per_batch, per_row):
    if per_row:
        (x_ref, g_ref, sh_ref, sc_ref, w_ref, wf_ref, bf_ref, wc_ref, s0_ref, s1_ref,
         q_ref, kf_ref, vf_ref, kb_ref, vb_ref, zc_ref, yb_ref, sga_ref, sgb_ref,
         lf16_ref, lf128_ref, h_scr) = refs
    else:
        (x_ref, g_ref, sh_ref, sc_ref, w_ref, wf_ref, bf_ref, wc_ref,
         q_ref, kf_ref, vf_ref, kb_ref, vb_ref, zc_ref, yb_ref, sga_ref, sgb_ref,
         lf16_ref, lf128_ref, h_scr, halo_scr) = refs
    i = pl.program_id(0)
    j = pl.program_id(1)

    @pl.when(j == 0)
    def _():
        h = _rms(x_ref[...], g_ref[...]) * (1.0 + _mod(sc_ref)) + _mod(sh_ref)
        hb = h.astype(BF16)
        h_scr[...] = hb
        f = jnp.dot(hb, wf_ref[...], preferred_element_type=F32)
        lf = _log_sigmoid(f + bf_ref[...])
        lf128_ref[...] = lf
        lf16_ref[...] = lf[:, :N_HEADS]

    res = jnp.dot(h_scr[...], w_ref[...], preferred_element_type=F32)
    cw = D_ATTN // N_CHUNK
    gw = D_MODEL // N_CHUNK
    u = res[:, 0:cw]
    cg = res[:, cw:2 * cw]
    bg = res[:, 2 * cw:3 * cw]
    ga = res[:, 3 * cw:3 * cw + gw]
    gb = res[:, 3 * cw + gw:3 * cw + 2 * gw]
    o = 3 * cw + 2 * gw
    q = res[:, o:o + cw]
    k = res[:, o + cw:o + 2 * cw]
    v = res[:, o + 2 * cw:o + 3 * cw]

    q_ref[0] = (q * ATTN_SCALE).astype(BF16)
    kf_ref[...] = k
    vf_ref[...] = v
    kb_ref[0] = k.astype(BF16)
    vb_ref[0] = v.astype(BF16)
    sga_ref[...] = jax.nn.sigmoid(ga).astype(BF16)
    sgb_ref[...] = jax.nn.sigmoid(gb).astype(BF16)

    zc = cg * u
    zc_ref[...] = zc
    row = lax.broadcasted_iota(jnp.int32, zc.shape, 0)
    r1 = pltpu.roll(zc, 1, 0)
    r2 = pltpu.roll(zc, 2, 0)
    if per_row:
        t = row & 3
        s0 = s0_ref[...]
        s1 = s1_ref[...]
        p1 = jnp.where(t >= 1, r1, s1)
        p2 = jnp.where(t >= 2, r2, jnp.where(t == 1, s1, s0))
    else:
        first = (i % blocks_per_batch) == 0
        halo = halo_scr[j]
        h6 = jnp.where(first, 0.0, halo[6:7, :])
        h7 = jnp.where(first, 0.0, halo[7:8, :])
        p1 = jnp.where(row == 0, h7, r1)
        p2 = jnp.where(row == 0, h6, jnp.where(row == 1, h7, r2))
        halo_scr[j] = zc[tm - 8:tm, :]
    wc = wc_ref[...]
    y = p2 * wc[0:1, :]
    y = y + p1 * wc[1:2, :]
    y = y + zc * wc[2:3, :]
    yb_ref[...] = (bg * y).astype(BF16)


def _in_proj(x, g_pre, shift, scale, w_main, w_f, b_f, w_conv, state, *, tm, rows_per_batch):
    n = x.shape[0]
    per_row = state is not None
    nblk = n // tm
    cw = D_ATTN // N_CHUNK
    gw = D_MODEL // N_CHUNK
    if per_row:
        bpb = 1
        mod_spec = lambda c: pl.BlockSpec((tm, D_MODEL), lambda i, j: (i, c))
    else:
        bpb = rows_per_batch // tm
        mod_spec = lambda c: pl.BlockSpec((1, 1, D_MODEL), lambda i, j: (i // bpb, 0, c))
    in_specs = [
        pl.BlockSpec((tm, D_MODEL), lambda i, j: (i, 0)),
        pl.BlockSpec((1, D_MODEL), lambda i, j: (0, 0)),
        mod_spec(0), mod_spec(1),
        pl.BlockSpec((D_MODEL, CHUNK_W), lambda i, j: (0, j)),
        pl.BlockSpec((D_MODEL, LANES), lambda i, j: (0, 0)),
        pl.BlockSpec((1, LANES), lambda i, j: (0, 0)),
        pl.BlockSpec((CONV_W, cw), lambda i, j: (0, j)),
    ]
    args = [x, g_pre, shift, scale, w_main, w_f, b_f, w_conv]
    scratch = [pltpu.VMEM((tm, D_MODEL), BF16)]
    if per_row:
        in_specs += [pl.BlockSpec((tm, cw), lambda i, j: (i, j))] * 2
        args += [state[0], state[1]]
    else:
        scratch.append(pltpu.VMEM((N_CHUNK, 8, cw), F32))
    hp_spec = pl.BlockSpec((1, tm, LANES), lambda i, j: (j, i, 0))
    col_spec = lambda w: pl.BlockSpec((tm, w), lambda i, j: (i, j))
    out_specs = [hp_spec, col_spec(cw), col_spec(cw), hp_spec, hp_spec,
                 col_spec(cw), col_spec(cw), col_spec(gw), col_spec(gw),
                 pl.BlockSpec((tm, N_HEADS), lambda i, j: (i, 0)),
                 pl.BlockSpec((tm, LANES), lambda i, j: (i, 0))]
    hp_shape = jax.ShapeDtypeStruct((N_HP, n, LANES), BF16)
    out_shape = [hp_shape,
                 jax.ShapeDtypeStruct((n, D_ATTN), F32), jax.ShapeDtypeStruct((n, D_ATTN), F32),
                 hp_shape, hp_shape,
                 jax.ShapeDtypeStruct((n, D_CONV), F32), jax.ShapeDtypeStruct((n, D_CONV), BF16),
                 jax.ShapeDtypeStruct((n, D_MODEL), BF16), jax.ShapeDtypeStruct((n, D_MODEL), BF16),
                 jax.ShapeDtypeStruct((n, N_HEADS), F32), jax.ShapeDtypeStruct((n, LANES), F32)]
    return pl.pallas_call(
        functools.partial(_in_proj_kernel, tm=tm, blocks_per_batch=bpb, per_row=per_row),
        grid=(nblk, N_CHUNK),
        in_specs=in_specs, out_specs=out_specs, out_shape=out_shape,
        scratch_shapes=scratch,
        compiler_params=_params(2),
        name="in_proj_sample" if per_row else "in_proj_prompt",
    )(*args)


def _lane_cumsum(x):
    n = x.shape[-1]
    lane = lax.broadcasted_iota(jnp.int32, x.shape, x.ndim - 1)
    s = 1
    while s < n:
        x = x + jnp.where(lane >= s, pltpu.roll(x, s, x.ndim - 1), 0.0)
        s *= 2
    return x


def _cumsum_kernel(lf_ref, cum_ref, cumt_ref):
    ct = _lane_cumsum(lf_ref[...].T)
    cumt_ref[0] = ct
    cum_ref[...] = ct.T


def _cumsum(lf128, n_batch, t):
    return pl.pallas_call(
        _cumsum_kernel,
        grid=(n_batch,),
        in_specs=[pl.BlockSpec((t, LANES), lambda b: (b, 0))],
        out_specs=[pl.BlockSpec((t, LANES), lambda b: (b, 0)),
                   pl.BlockSpec((1, LANES, t), lambda b: (b, 0, 0))],
        out_shape=[jax.ShapeDtypeStruct((n_batch * t, LANES), F32),
                   jax.ShapeDtypeStruct((n_batch, LANES, t), F32)],
        compiler_params=_params(1),
        name="cumsum",
    )(lf128)


def _prompt_attn_kernel(qi_ref, kj_ref, q_ref, k_ref, v_ref, cq_ref, ck_ref, o_ref,
                        acc_scr, m_scr, l_scr, *, tq):
    t = pl.program_id(1)
    qi = qi_ref[t]
    kj = kj_ref[t]

    @pl.when(kj == 0)
    def _():
        acc_scr[...] = jnp.zeros_like(acc_scr)
        m_scr[...] = jnp.full_like(m_scr, -jnp.inf)
        l_scr[...] = jnp.zeros_like(l_scr)

    lane = lax.broadcasted_iota(jnp.int32, (1, LANES), 1)
    lo_half = lane < HEAD_DIM

    def head_pair(hp, carry, *, masked):
        q2 = q_ref[hp]
        k2 = k_ref[hp]
        v2 = v_ref[hp]
        cq_all = cq_ref[...]
        pvs = []
        alphas = []
        for e in range(2):
            head = 2 * hp + e
            sel = lo_half if e == 0 else jnp.logical_not(lo_half)
            qm = jnp.where(sel, q2, jnp.zeros_like(q2))
            s = lax.dot_general(qm, k2, (((1,), (1,)), ((), ())),
                                preferred_element_type=F32)
            ck = ck_ref[0, pl.ds(head, 1), :]
            cq = jnp.sum(jnp.where(lane == head, cq_all, 0.0), axis=1, keepdims=True)
            lg = s - ck
            if masked:
                row = lax.broadcasted_iota(jnp.int32, lg.shape, 0)
                col = lax.broadcasted_iota(jnp.int32, lg.shape, 1)
                lg = jnp.where(col <= row, lg, -jnp.inf)
            m_prev = m_scr[head]
            m_new = jnp.maximum(m_prev, jnp.max(lg, axis=1, keepdims=True) + cq)
            p = jnp.exp(lg + (cq - m_new))
            alpha = jnp.exp(m_prev - m_new)
            l_scr[head] = alpha * l_scr[head] + jnp.sum(p, axis=1, keepdims=True)
            m_scr[head] = m_new
            vm = jnp.where(sel, v2, jnp.zeros_like(v2))
            pvs.append(jnp.dot(p.astype(BF16), vm, preferred_element_type=F32))
            alphas.append(alpha)
        alpha2 = jnp.where(lo_half, alphas[0], alphas[1])
        acc_scr[hp] = acc_scr[hp] * alpha2 + pvs[0] + pvs[1]
        return carry

    @pl.when(kj < qi)
    def _():
        lax.fori_loop(0, N_HP, functools.partial(head_pair, masked=False), 0)

    @pl.when(kj == qi)
    def _():
        lax.fori_loop(0, N_HP, functools.partial(head_pair, masked=True), 0)

        def fin(hp, carry):
            inv = jnp.where(lo_half, 1.0 / l_scr[2 * hp], 1.0 / l_scr[2 * hp + 1])
            o_ref[hp] = (acc_scr[hp] * inv).astype(BF16)
            return carry
        lax.fori_loop(0, N_HP, fin, 0)


def _prompt_attn(q_hp, k_hp, v_hp, cum, cumt, n_batch, t, tq):
    nq = t // tq
    pairs = [(a, b) for a in range(nq) for b in range(a + 1)]
    qi_tab = jnp.asarray([p[0] for p in pairs], jnp.int32)
    kj_tab = jnp.asarray([p[1] for p in pairs], jnp.int32)
    blk = (N_HP, tq, LANES)
    grid_spec = pltpu.PrefetchScalarGridSpec(
        num_scalar_prefetch=2,
        grid=(n_batch, len(pairs)),
        in_specs=[
            pl.BlockSpec(blk, lambda b, s, qi, kj: (0, b * nq + qi[s], 0)),
            pl.BlockSpec(blk, lambda b, s, qi, kj: (0, b * nq + kj[s], 0)),
            pl.BlockSpec(blk, lambda b, s, qi, kj: (0, b * nq + kj[s], 0)),
            pl.BlockSpec((tq, LANES), lambda b, s, qi, kj: (b * nq + qi[s], 0)),
            pl.BlockSpec((1, N_HEADS, tq), lambda b, s, qi, kj: (b, 0, kj[s])),
        ],
        out_specs=pl.BlockSpec(blk, lambda b, s, qi, kj: (0, b * nq + qi[s], 0)),
        scratch_shapes=[pltpu.VMEM((N_HP, tq, LANES), F32),
                        pltpu.VMEM((N_HEADS, tq, 1), F32),
                        pltpu.VMEM((N_HEADS, tq, 1), F32)],
    )
    return pl.pallas_call(
        functools.partial(_prompt_attn_kernel, tq=tq),
        grid_spec=grid_spec,
        out_shape=jax.ShapeDtypeStruct((N_HP, n_batch * t, LANES), BF16),
        compiler_params=_params(2),
        name="prompt_attn",
    )(qi_tab, kj_tab, q_hp, k_hp, v_hp, cum, cumt)


def _sample_attn_kernel(*refs, n_pages, n_new):
    pt_ref = refs[0]
    q_ref, kn_ref, vn_ref, lfn_ref = refs[1:5]
    k_refs = refs[5:5 + n_pages]
    v_refs = refs[5 + n_pages:5 + 2 * n_pages]
    lf_refs = refs[5 + 2 * n_pages:5 + 3 * n_pages]
    o_ref = refs[5 + 3 * n_pages]
    pad_scr = refs[6 + 3 * n_pages]
    del pt_ref
    nr = n_new * N_HEADS
    nt = ((1,), (1,)), ((), ())

    q = q_ref[0]
    hrow = lax.broadcasted_iota(jnp.int32, (N_HEADS, D_ATTN), 0)
    hlane = lax.broadcasted_iota(jnp.int32, (N_HEADS, D_ATTN), 1) // HEAD_DIM
    hmask = hrow == hlane
    qbd = jnp.concatenate(
        [jnp.where(hmask, jnp.broadcast_to(q[qq:qq + 1, :], (N_HEADS, D_ATTN)), 0.0)
         for qq in range(n_new)], axis=0).astype(BF16)

    def head_major(lf):
        pad_scr[...] = jnp.zeros_like(pad_scr)
        pad_scr[0:lf.shape[0], 0:N_HEADS] = lf
        return pad_scr[...].T[0:N_HEADS, :]

    s_parts = []
    lf_parts = []
    for c in range(n_pages):
        kb = k_refs[c][0].astype(BF16)
        s_parts.append(lax.dot_general(qbd, kb, nt, preferred_element_type=F32))
        lf_parts.append(head_major(lf_refs[c][0]))
    s_past = jnp.concatenate(s_parts, axis=1)
    ck_past = _lane_cumsum(jnp.concatenate(lf_parts, axis=1))
    n_past = ck_past.shape[1]
    c_total = ck_past[:, n_past - 1:n_past]

    kn = jnp.concatenate(
        [kn_ref[0], jnp.zeros((PAGE_SIZE - 8, D_ATTN), F32)], axis=0).astype(BF16)
    vn = jnp.concatenate(
        [vn_ref[0], jnp.zeros((PAGE_SIZE - 8, D_ATTN), F32)], axis=0).astype(BF16)
    s_new = lax.dot_general(qbd, kn, nt, preferred_element_type=F32)
    ck_new = c_total + _lane_cumsum(head_major(lfn_ref[0]))

    tile = lambda a: jnp.concatenate([a] * n_new, axis=0)
    ck_past_r = tile(ck_past)
    ck_new_r = tile(ck_new)
    qq_row = lax.broadcasted_iota(jnp.int32, (nr, PAGE_SIZE), 0) // N_HEADS
    lane = lax.broadcasted_iota(jnp.int32, (nr, PAGE_SIZE), 1)
    cq = jnp.sum(jnp.where(lane == qq_row, ck_new_r, 0.0), axis=1, keepdims=True)

    lg_past = s_past + (cq - ck_past_r)
    lg_new = jnp.where(lane <= qq_row, s_new + (cq - ck_new_r), -jnp.inf)
    m = jnp.maximum(jnp.max(lg_past, axis=1, keepdims=True),
                    jnp.max(lg_new, axis=1, keepdims=True))
    p_past = jnp.exp(lg_past - m)
    p_new = jnp.exp(lg_new - m)
    denom = jnp.sum(p_past, axis=1, keepdims=True) + jnp.sum(p_new, axis=1, keepdims=True)
    acc = jnp.dot(p_new.astype(BF16), vn, preferred_element_type=F32)
    pb = p_past.astype(BF16)
    for c in range(n_pages):
        vb = v_refs[c][0].astype(BF16)
        acc = acc + jnp.dot(pb[:, c * PAGE_SIZE:(c + 1) * PAGE_SIZE], vb,
                            preferred_element_type=F32)
    acc = acc / denom
    orow = lax.broadcasted_iota(jnp.int32, (8, D_ATTN), 0)
    out = jnp.zeros((8, D_ATTN), F32)
    for qq in range(n_new):
        blk = jnp.where(hmask, acc[qq * N_HEADS:(qq + 1) * N_HEADS, :], 0.0)
        out = jnp.where(orow == qq, jnp.sum(blk, axis=0, keepdims=True), out)
    o_ref[0] = out


def _sample_attn(page_table, q8, kn8, vn8, lfn8, cache_k, cache_v, cache_lf, n_new):
    n_batch, n_pages = page_table.shape
    row_spec = lambda w: pl.BlockSpec((1, 8, w), lambda b, pt: (b, 0, 0))

    def page_spec(c, w):
        return pl.BlockSpec((1, PAGE_SIZE, w), lambda b, pt: (pt[b * n_pages + c], 0, 0))

    in_specs = [row_spec(D_ATTN), row_spec(D_ATTN), row_spec(D_ATTN), row_spec(N_HEADS)]
    in_specs += [page_spec(c, D_ATTN) for c in range(n_pages)]
    in_specs += [page_spec(c, D_ATTN) for c in range(n_pages)]
    in_specs += [page_spec(c, N_HEADS) for c in range(n_pages)]
    grid_spec = pltpu.PrefetchScalarGridSpec(
        num_scalar_prefetch=1,
        grid=(n_batch,),
        in_specs=in_specs,
        out_specs=row_spec(D_ATTN),
        scratch_shapes=[pltpu.VMEM((PAGE_SIZE, LANES), F32)],
    )
    return pl.pallas_call(
        functools.partial(_sample_attn_kernel, n_pages=n_pages, n_new=n_new),
        grid_spec=grid_spec,
        out_shape=jax.ShapeDtypeStruct((n_batch, 8, D_ATTN), F32),
        compiler_params=_params(1),
        name="sample_attn",
    )(page_table.reshape(-1), q8, kn8, vn8, lfn8,
      *([cache_k] * n_pages), *([cache_v] * n_pages), *([cache_lf] * n_pages))


def _merge_kernel(o_ref, yb_ref, sga_ref, sgb_ref, x_ref, gt_ref, sh_ref, sc_ref,
                  gp1_ref, gp2_ref, woa_ref, wob_ref, wo_ref, x1_ref, h2_ref):
    o = jnp.concatenate([o_ref[hp] for hp in range(N_HP)], axis=1)
    a = jnp.dot(o, woa_ref[...], preferred_element_type=F32)
    b = jnp.dot(yb_ref[...], wob_ref[...], preferred_element_type=F32)
    mm = (sga_ref[...].astype(F32) * a + sgb_ref[...].astype(F32) * b).astype(BF16)
    m = jnp.dot(mm, wo_ref[...], preferred_element_type=F32)
    x1 = x_ref[...] + _mod(gt_ref) * _rms(m, gp1_ref[...])
    x1_ref[...] = x1
    h2 = _rms(x1, gp2_ref[...]) * (1.0 + _mod(sc_ref)) + _mod(sh_ref)
    h2_ref[...] = h2.astype(BF16)


def _merge(o_hp, yb, sga, sgb, x, mods, g_post1, g_pre2, w_oa, w_ob, w_o, *, tm,
           rows_per_batch, per_row):
    n = x.shape[0]
    if per_row:
        mod_spec = lambda c: pl.BlockSpec((tm, D_MODEL), lambda i: (i, c))
    else:
        bpb = rows_per_batch // tm
        mod_spec = lambda c: pl.BlockSpec((1, 1, D_MODEL), lambda i: (i // bpb, 0, c))
    row_spec = lambda w: pl.BlockSpec((tm, w), lambda i: (i, 0))
    const = lambda shape: pl.BlockSpec(shape, lambda i: (0,) * len(shape),
                                       pipeline_mode=pl.Buffered(1))
    return pl.pallas_call(
        _merge_kernel,
        grid=(n // tm,),
        in_specs=[pl.BlockSpec((N_HP, tm, LANES), lambda i: (0, i, 0)),
                  row_spec(D_CONV), row_spec(D_MODEL), row_spec(D_MODEL), row_spec(D_MODEL),
                  mod_spec(2), mod_spec(3), mod_spec(4),
                  const((1, D_MODEL)), const((1, D_MODEL)),
                  const((D_ATTN, D_MODEL)), const((D_CONV, D_MODEL)),
                  const((D_MODEL, D_MODEL))],
        out_specs=[row_spec(D_MODEL), row_spec(D_MODEL)],
        out_shape=[jax.ShapeDtypeStruct((n, D_MODEL), F32),
                   jax.ShapeDtypeStruct((n, D_MODEL), BF16)],
        compiler_params=_params(1),
        name="merge_sample" if per_row else "merge_prompt",
    )(o_hp, yb, sga, sgb, x, mods, mods, mods, g_post1, g_pre2, w_oa, w_ob, w_o)


def _ffn_kernel(h_ref, wg_ref, wu_ref, wout_ref, x1_ref, gt_ref, gp_ref, y_ref, acc_scr):
    j = pl.program_id(1)
    h = h_ref[...]
    g = jnp.dot(h, wg_ref[...], preferred_element_type=F32)
    u = jnp.dot(h, wu_ref[...], preferred_element_type=F32)
    act = (g * jax.nn.sigmoid(g) * u).astype(BF16)
    part = jnp.dot(act, wout_ref[...], preferred_element_type=F32)

    @pl.when(j == 0)
    def _():
        acc_scr[...] = part

    @pl.when(j > 0)
    def _():
        acc_scr[...] += part

    @pl.when(j == pl.num_programs(1) - 1)
    def _():
        y_ref[...] = x1_ref[...] + _mod(gt_ref) * _rms(acc_scr[...], gp_ref[...])


def _ffn(h2, x1, mods, g_post2, w_ffn_in, w_ffn_out, *, tm, tf, rows_per_batch, per_row):
    n = x1.shape[0]
    nf = D_FF // tf
    if per_row:
        mod_spec = pl.BlockSpec((tm, D_MODEL), lambda i, j: (i, 5))
    else:
        bpb = rows_per_batch // tm
        mod_spec = pl.BlockSpec((1, 1, D_MODEL), lambda i, j: (i // bpb, 0, 5))
    row_spec = pl.BlockSpec((tm, D_MODEL), lambda i, j: (i, 0))
    return pl.pallas_call(
        _ffn_kernel,
        grid=(n // tm, nf),
        in_specs=[row_spec,
                  pl.BlockSpec((D_MODEL, tf), lambda i, j: (0, j)),
                  pl.BlockSpec((D_MODEL, tf), lambda i, j: (0, j + nf)),
                  pl.BlockSpec((tf, D_MODEL), lambda i, j: (j, 0)),
                  row_spec, mod_spec,
                  pl.BlockSpec((1, D_MODEL), lambda i, j: (0, 0))],
        out_specs=row_spec,
        out_shape=jax.ShapeDtypeStruct((n, D_MODEL), F32),
        scratch_shapes=[pltpu.VMEM((tm, D_MODEL), F32)],
        compiler_params=_params(2),
        name="ffn_sample" if per_row else "ffn_prompt",
    )(h2, w_ffn_in, w_ffn_in, w_ffn_out, x1, mods, g_post2)


def _chunked_in_weights(w_in):
    offs = np.cumsum([0, D_ATTN, D_ATTN, D_ATTN, N_HEADS, D_CONV, D_CONV, D_CONV,
                      D_MODEL, D_MODEL])
    part = lambda idx: w_in[:, offs[idx]:offs[idx + 1]]
    q, k, v, f, u, bg, cg, ga, gb = [part(t) for t in range(9)]
    d = w_in.shape[0]
    ch = lambda a: a.reshape(d, N_CHUNK, -1)
    w_main = jnp.concatenate([ch(u), ch(cg), ch(bg), ch(ga), ch(gb), ch(q), ch(k), ch(v)],
                             axis=2).reshape(d, N_CHUNK * CHUNK_W).astype(BF16)
    w_f = jnp.pad(f, ((0, 0), (0, LANES - N_HEADS))).astype(BF16)
    return w_main, w_f


def kernel(x_prompt, x_sample, cache_k, cache_v, cache_logf, state_conv, page_table,
           c_prompt, c_sample, w_ada, b_ada, g_pre1, w_in, b_f, w_conv, w_oa, w_ob, w_o,
           g_post1, g_pre2, w_ffn_in, w_ffn_out, g_post2):
    depth = w_in.shape[0]
    assert depth == 1
    nb, t, d = x_prompt.shape
    db, ts, _ = x_sample.shape
    n_pool = cache_k.shape[1]
    l = 0

    w_main, w_f = _chunked_in_weights(w_in[l])
    b_f_pad = jnp.pad(b_f[l], (0, LANES - N_HEADS)).reshape(1, LANES)
    w_oa_b = w_oa[l].astype(BF16)
    w_ob_b = w_ob[l].astype(BF16)
    w_o_b = w_o[l].astype(BF16)
    w_fin_b = w_ffn_in[l].astype(BF16)
    w_fout_b = w_ffn_out[l].astype(BF16)
    row2 = lambda a: a.reshape(1, -1)

    mods = _adaln(jnp.concatenate([c_prompt, c_sample], axis=0), w_ada[l], b_ada[l])
    mods_p = mods[:nb].reshape(nb, 1, 6 * d)
    mods_s = jnp.repeat(mods[nb:], ts, axis=0)

    xp = x_prompt.reshape(nb * t, d)
    (q_hp, k_f, v_f, k_hp, v_hp, zc, yb, sga, sgb, lf16, lf128) = _in_proj(
        xp, row2(g_pre1[l]), mods_p, mods_p, w_main, w_f, b_f_pad, w_conv[l], None,
        tm=1024, rows_per_batch=t)
    cum, cumt = _cumsum(lf128, nb, t)
    o_hp = _prompt_attn(q_hp, k_hp, v_hp, cum, cumt, nb, t, 512)
    x1, h2 = _merge(o_hp, yb, sga, sgb, xp, mods_p, row2(g_post1[l]), row2(g_pre2[l]),
                    w_oa_b, w_ob_b, w_o_b, tm=512, rows_per_batch=t, per_row=False)
    yp = _ffn(h2, x1, mods_p, row2(g_post2[l]), w_fin_b, w_fout_b, tm=512, tf=512,
              rows_per_batch=t, per_row=False)

    ns = db * ts
    xs = x_sample.reshape(ns, d)
    st = state_conv[l].astype(F32)
    state = (jnp.repeat(st[:, 0, :], ts, axis=0), jnp.repeat(st[:, 1, :], ts, axis=0))
    (qs_hp, ks_f, vs_f, _, _, zcs, ybs, sgas, sgbs, lfs16, _) = _in_proj(
        xs, row2(g_pre1[l]), mods_s, mods_s, w_main, w_f, b_f_pad, w_conv[l], state,
        tm=ns, rows_per_batch=ts)
    pad8 = lambda a: jnp.pad(a.reshape(db, ts, -1), ((0, 0), (0, 8 - ts), (0, 0)))
    qs = qs_hp.transpose(1, 0, 2).reshape(ns, D_ATTN).astype(F32)
    os8 = _sample_attn(page_table, pad8(qs), pad8(ks_f), pad8(vs_f), pad8(lfs16),
                       cache_k[l].reshape(n_pool, PAGE_SIZE, D_ATTN),
                       cache_v[l].reshape(n_pool, PAGE_SIZE, D_ATTN),
                       cache_logf[l], ts)
    os_hp = (os8[:, :ts, :].reshape(ns, N_HP, LANES).transpose(1, 0, 2).astype(BF16))
    x1s, h2s = _merge(os_hp, ybs, sgas, sgbs, xs, mods_s, row2(g_post1[l]), row2(g_pre2[l]),
                      w_oa_b, w_ob_b, w_o_b, tm=ns, rows_per_batch=ts, per_row=True)
    ys = _ffn(h2s, x1s, mods_s, row2(g_post2[l]), w_fin_b, w_fout_b, tm=ns, tf=512,
              rows_per_batch=ts, per_row=True)

    heads = lambda a, b_, t_: a.reshape(1, b_, t_, N_HEADS, HEAD_DIM)
    return (yp.reshape(nb, t, d), ys.reshape(db, ts, d),
            heads(k_f, nb, t), heads(v_f, nb, t), lf16.reshape(1, nb, t, N_HEADS),
            zc.reshape(nb, t, D_CONV)[:, t - (CONV_W - 1):, :][None],
            heads(ks_f, db, ts), heads(vs_f, db, ts), lfs16.reshape(1, db, ts, N_HEADS),
            zcs.reshape(db, ts, D_CONV)[:, ts - (CONV_W - 1):, :][None])
```

```python
import functools

import jax
import jax.numpy as jnp
import numpy as np
from jax import lax
from jax.experimental import pallas as pl
from jax.experimental.pallas import tpu as pltpu

F32 = jnp.float32
BF16 = jnp.bfloat16

D_MODEL = 2048
N_HEADS = 16
HEAD_DIM = 64
D_ATTN = N_HEADS * HEAD_DIM
D_CONV = 1024
CONV_W = 3
D_FF = 5632
RMS_EPS = 1e-6
ATTN_SCALE = HEAD_DIM ** -0.5
PAGE_SIZE = 128

LANES = 128
N_HP = D_ATTN // LANES
N_CHUNK = 8
CHUNK_W = 6 * (D_ATTN // N_CHUNK) + 2 * (D_MODEL // N_CHUNK)
VMEM_LIMIT = 56 * 1024 * 1024


def _params(n_axes, vmem=VMEM_LIMIT):
    return pltpu.CompilerParams(
        dimension_semantics=("arbitrary",) * n_axes, vmem_limit_bytes=vmem)


def _rms(x, g):
    return x * lax.rsqrt(jnp.mean(x * x, axis=-1, keepdims=True) + RMS_EPS) * g


def _mod(ref):
    return ref[0] if len(ref.shape) == 3 else ref[...]


def _log_sigmoid(z):
    return jnp.minimum(z, 0.0) - jnp.log1p(jnp.exp(-jnp.abs(z)))


def _adaln_kernel(c_ref, w_ref, b_ref, o_ref):
    c = c_ref[...]
    s = (c * jax.nn.sigmoid(c)).astype(BF16)
    o_ref[...] = jnp.dot(s, w_ref[...].astype(BF16),
                         preferred_element_type=F32) + b_ref[...]


def _adaln(c, w_ada, b_ada):
    m, d = c.shape
    n = w_ada.shape[1]
    tn = 1024
    return pl.pallas_call(
        _adaln_kernel,
        grid=(n // tn,),
        in_specs=[pl.BlockSpec((m, d), lambda j: (0, 0)),
                  pl.BlockSpec((d, tn), lambda j: (0, j)),
                  pl.BlockSpec((1, tn), lambda j: (0, j))],
        out_specs=pl.BlockSpec((m, tn), lambda j: (0, j)),
        out_shape=jax.ShapeDtypeStruct((m, n), F32),
        compiler_params=_params(1),
        name="adaln",
    )(c, w_ada, b_ada.reshape(1, n))


def _in_proj_kernel(*refs, tm, blocks_per_batch, per_row):
    if per_row:
        (x_ref, g_ref, sh_ref, sc_ref, w_ref, wf_ref, bf_ref, wc_ref, s0_ref, s1_ref,
         q_ref, kt_ref, vt_ref, kb_ref, vb_ref, zc_ref, yb_ref, sga_ref, sgb_ref,
         lft_ref, h_scr) = refs
    else:
        (x_ref, g_ref, sh_ref, sc_ref, w_ref, wf_ref, bf_ref, wc_ref,
         q_ref, kt_ref, vt_ref, kb_ref, vb_ref, zc_ref, yb_ref, sga_ref, sgb_ref,
         lft_ref, h_scr, halo_scr) = refs
    i = pl.program_id(0)
    j = pl.program_id(1)

    @pl.when(j == 0)
    def _():
        h = _rms(x_ref[...], g_ref[...]) * (1.0 + _mod(sc_ref)) + _mod(sh_ref)
        hb = h.astype(BF16)
        h_scr[...] = hb
        f = jnp.dot(hb, wf_ref[...], preferred_element_type=F32)
        lf = _log_sigmoid(f + bf_ref[...])
        lft_ref[0] = lf.T[0:N_HEADS, :]

    res = jnp.dot(h_scr[...], w_ref[...], preferred_element_type=F32)
    cw = D_ATTN // N_CHUNK
    gw = D_MODEL // N_CHUNK
    u = res[:, 0:cw]
    cg = res[:, cw:2 * cw]
    bg = res[:, 2 * cw:3 * cw]
    ga = res[:, 3 * cw:3 * cw + gw]
    gb = res[:, 3 * cw + gw:3 * cw + 2 * gw]
    o = 3 * cw + 2 * gw
    q = res[:, o:o + cw]
    k = res[:, o + cw:o + 2 * cw]
    v = res[:, o + 2 * cw:o + 3 * cw]

    q_ref[0] = (q * ATTN_SCALE).astype(BF16)
    kt_ref[0] = k.T
    vt_ref[0] = v.T
    kb_ref[0] = k.astype(BF16)
    vb_ref[0] = v.astype(BF16)
    sga_ref[...] = jax.nn.sigmoid(ga).astype(BF16)
    sgb_ref[...] = jax.nn.sigmoid(gb).astype(BF16)

    zc = cg * u
    zc_ref[...] = zc
    row = lax.broadcasted_iota(jnp.int32, zc.shape, 0)
    r1 = pltpu.roll(zc, 1, 0)
    r2 = pltpu.roll(zc, 2, 0)
    if per_row:
        t = row & 3
        s0 = s0_ref[...]
        s1 = s1_ref[...]
        p1 = jnp.where(t >= 1, r1, s1)
        p2 = jnp.where(t >= 2, r2, jnp.where(t == 1, s1, s0))
    else:
        first = (i % blocks_per_batch) == 0
        halo = halo_scr[j]
        h6 = jnp.where(first, 0.0, halo[6:7, :])
        h7 = jnp.where(first, 0.0, halo[7:8, :])
        p1 = jnp.where(row == 0, h7, r1)
        p2 = jnp.where(row == 0, h6, jnp.where(row == 1, h7, r2))
        halo_scr[j] = zc[tm - 8:tm, :]
    wc = wc_ref[...]
    y = p2 * wc[0:1, :]
    y = y + p1 * wc[1:2, :]
    y = y + zc * wc[2:3, :]
    yb_ref[...] = (bg * y).astype(BF16)


def _in_proj(x, g_pre, shift, scale, w_main, w_f, b_f, w_conv, state, *, tm, rows_per_batch):
    n = x.shape[0]
    per_row = state is not None
    nblk = n // tm
    cw = D_ATTN // N_CHUNK
    gw = D_MODEL // N_CHUNK
    if per_row:
        bpb = 1
        mod_spec = lambda c: pl.BlockSpec((tm, D_MODEL), lambda i, j: (i, c))
    else:
        bpb = rows_per_batch // tm
        mod_spec = lambda c: pl.BlockSpec((1, 1, D_MODEL), lambda i, j: (i // bpb, 0, c))
    in_specs = [
        pl.BlockSpec((tm, D_MODEL), lambda i, j: (i, 0)),
        pl.BlockSpec((1, D_MODEL), lambda i, j: (0, 0)),
        mod_spec(0), mod_spec(1),
        pl.BlockSpec((D_MODEL, CHUNK_W), lambda i, j: (0, j)),
        pl.BlockSpec((D_MODEL, LANES), lambda i, j: (0, 0)),
        pl.BlockSpec((1, LANES), lambda i, j: (0, 0)),
        pl.BlockSpec((CONV_W, cw), lambda i, j: (0, j)),
    ]
    args = [x, g_pre, shift, scale, w_main, w_f, b_f, w_conv]
    scratch = [pltpu.VMEM((tm, D_MODEL), BF16)]
    if per_row:
        in_specs += [pl.BlockSpec((tm, cw), lambda i, j: (i, j))] * 2
        args += [state[0], state[1]]
    else:
        scratch.append(pltpu.VMEM((N_CHUNK, 8, cw), F32))
    hp_spec = pl.BlockSpec((1, tm, LANES), lambda i, j: (j, i, 0))
    col_spec = lambda w: pl.BlockSpec((tm, w), lambda i, j: (i, j))
    nbat = nblk // bpb
    t_spec = lambda rows: pl.BlockSpec(
        (1, rows, tm), lambda i, j: (i // bpb, j if rows == LANES else 0, i % bpb))
    t_shape = lambda rows: jax.ShapeDtypeStruct((nbat, rows, bpb * tm), F32)
    out_specs = [hp_spec, t_spec(LANES), t_spec(LANES), hp_spec, hp_spec,
                 col_spec(cw), col_spec(cw), col_spec(gw), col_spec(gw),
                 t_spec(N_HEADS)]
    hp_shape = jax.ShapeDtypeStruct((N_HP, n, LANES), BF16)
    out_shape = [hp_shape, t_shape(D_ATTN), t_shape(D_ATTN), hp_shape, hp_shape,
                 jax.ShapeDtypeStruct((n, D_CONV), F32), jax.ShapeDtypeStruct((n, D_CONV), BF16),
                 jax.ShapeDtypeStruct((n, D_MODEL), BF16), jax.ShapeDtypeStruct((n, D_MODEL), BF16),
                 t_shape(N_HEADS)]
    return pl.pallas_call(
        functools.partial(_in_proj_kernel, tm=tm, blocks_per_batch=bpb, per_row=per_row),
        grid=(nblk, N_CHUNK),
        in_specs=in_specs, out_specs=out_specs, out_shape=out_shape,
        scratch_shapes=scratch,
        compiler_params=_params(2),
        name="in_proj_sample" if per_row else "in_proj_prompt",
    )(*args)


def _lane_cumsum(x):
    n = x.shape[-1]
    lane = lax.broadcasted_iota(jnp.int32, x.shape, x.ndim - 1)
    s = 1
    while s < n:
        x = x + jnp.where(lane >= s, pltpu.roll(x, s, x.ndim - 1), 0.0)
        s *= 2
    return x


def _cumsum_kernel(lft_ref, cum_ref, cumt_ref):
    ct = _lane_cumsum(lft_ref[0])
    cumt_ref[0] = ct
    t = ct.shape[1]
    cum_ref[...] = jnp.concatenate([ct, jnp.zeros((LANES - N_HEADS, t), F32)], axis=0).T


def _cumsum(lft, n_batch, t):
    return pl.pallas_call(
        _cumsum_kernel,
        grid=(n_batch,),
        in_specs=[pl.BlockSpec((1, N_HEADS, t), lambda b: (b, 0, 0))],
        out_specs=[pl.BlockSpec((t, LANES), lambda b: (b, 0)),
                   pl.BlockSpec((1, N_HEADS, t), lambda b: (b, 0, 0))],
        out_shape=[jax.ShapeDtypeStruct((n_batch * t, LANES), F32),
                   jax.ShapeDtypeStruct((n_batch, N_HEADS, t), F32)],
        compiler_params=_params(1),
        name="cumsum",
    )(lft)


def _prompt_attn_kernel(qi_ref, kj_ref, q_ref, k_ref, v_ref, cq_ref, ck_ref, o_ref,
                        acc_scr, m_scr, l_scr, *, tq):
    t = pl.program_id(1)
    qi = qi_ref[t]
    kj = kj_ref[t]

    @pl.when(kj == 0)
    def _():
        acc_scr[...] = jnp.zeros_like(acc_scr)
        m_scr[...] = jnp.full_like(m_scr, -jnp.inf)
        l_scr[...] = jnp.zeros_like(l_scr)

    lane = lax.broadcasted_iota(jnp.int32, (1, LANES), 1)
    lo_half = lane < HEAD_DIM

    def head_pair(hp, carry, *, masked):
        q2 = q_ref[hp]
        k2 = k_ref[hp]
        v2 = v_ref[hp]
        cq_all = cq_ref[...]
        pvs = []
        alphas = []
        for e in range(2):
            head = 2 * hp + e
            sel = lo_half if e == 0 else jnp.logical_not(lo_half)
            qm = jnp.where(sel, q2, jnp.zeros_like(q2))
            s = lax.dot_general(qm, k2, (((1,), (1,)), ((), ())),
                                preferred_element_type=F32)
            ck = ck_ref[0, pl.ds(head, 1), :]
            cq = jnp.sum(jnp.where(lane == head, cq_all, 0.0), axis=1, keepdims=True)
            lg = s - ck
            if masked:
                row = lax.broadcasted_iota(jnp.int32, lg.shape, 0)
                col = lax.broadcasted_iota(jnp.int32, lg.shape, 1)
                lg = jnp.where(col <= row, lg, -jnp.inf)
            m_prev = m_scr[head]
            m_new = jnp.maximum(m_prev, jnp.max(lg, axis=1, keepdims=True) + cq)
            p = jnp.exp(lg + (cq - m_new))
            alpha = jnp.exp(m_prev - m_new)
            l_scr[head] = alpha * l_scr[head] + jnp.sum(p, axis=1, keepdims=True)
            m_scr[head] = m_new
            vm = jnp.where(sel, v2, jnp.zeros_like(v2))
            pvs.append(jnp.dot(p.astype(BF16), vm, preferred_element_type=F32))
            alphas.append(alpha)
        alpha2 = jnp.where(lo_half, alphas[0], alphas[1])
        acc_scr[hp] = acc_scr[hp] * alpha2 + pvs[0] + pvs[1]
        return carry

    @pl.when(kj < qi)
    def _():
        lax.fori_loop(0, N_HP, functools.partial(head_pair, masked=False), 0)

    @pl.when(kj == qi)
    def _():
        lax.fori_loop(0, N_HP, functools.partial(head_pair, masked=True), 0)

        def fin(hp, carry):
            inv = jnp.where(lo_half, 1.0 / l_scr[2 * hp], 1.0 / l_scr[2 * hp + 1])
            o_ref[hp] = (acc_scr[hp] * inv).astype(BF16)
            return carry
        lax.fori_loop(0, N_HP, fin, 0)


def _prompt_attn(q_hp, k_hp, v_hp, cum, cumt, n_batch, t, tq):
    nq = t // tq
    pairs = [(a, b) for a in range(nq) for b in range(a + 1)]
    qi_tab = jnp.asarray([p[0] for p in pairs], jnp.int32)
    kj_tab = jnp.asarray([p[1] for p in pairs], jnp.int32)
    blk = (N_HP, tq, LANES)
    grid_spec = pltpu.PrefetchScalarGridSpec(
        num_scalar_prefetch=2,
        grid=(n_batch, len(pairs)),
        in_specs=[
            pl.BlockSpec(blk, lambda b, s, qi, kj: (0, b * nq + qi[s], 0)),
            pl.BlockSpec(blk, lambda b, s, qi, kj: (0, b * nq + kj[s], 0)),
            pl.BlockSpec(blk, lambda b, s, qi, kj: (0, b * nq + kj[s], 0)),
            pl.BlockSpec((tq, LANES), lambda b, s, qi, kj: (b * nq + qi[s], 0)),
            pl.BlockSpec((1, N_HEADS, tq), lambda b, s, qi, kj: (b, 0, kj[s])),
        ],
        out_specs=pl.BlockSpec(blk, lambda b, s, qi, kj: (0, b * nq + qi[s], 0)),
        scratch_shapes=[pltpu.VMEM((N_HP, tq, LANES), F32),
                        pltpu.VMEM((N_HEADS, tq, 1), F32),
                        pltpu.VMEM((N_HEADS, tq, 1), F32)],
    )
    return pl.pallas_call(
        functools.partial(_prompt_attn_kernel, tq=tq),
        grid_spec=grid_spec,
        out_shape=jax.ShapeDtypeStruct((N_HP, n_batch * t, LANES), BF16),
        compiler_params=_params(2),
        name="prompt_attn",
    )(qi_tab, kj_tab, q_hp, k_hp, v_hp, cum, cumt)


def _sample_attn_kernel(*refs, n_pages, n_new):
    pt_ref = refs[0]
    q_ref, kn_ref, vn_ref, lfn_ref = refs[1:5]
    k_refs = refs[5:5 + n_pages]
    v_refs = refs[5 + n_pages:5 + 2 * n_pages]
    lf_refs = refs[5 + 2 * n_pages:5 + 3 * n_pages]
    o_ref = refs[5 + 3 * n_pages]
    del pt_ref
    nr = n_new * N_HEADS
    nt = ((1,), (1,)), ((), ())
    page_t = lambda ref: ref[0].reshape(D_ATTN, PAGE_SIZE).astype(BF16)

    q = q_ref[0]
    hrow = lax.broadcasted_iota(jnp.int32, (N_HEADS, D_ATTN), 0)
    hlane = lax.broadcasted_iota(jnp.int32, (N_HEADS, D_ATTN), 1) // HEAD_DIM
    hmask = hrow == hlane
    qbd = jnp.concatenate(
        [jnp.where(hmask, jnp.broadcast_to(q[qq:qq + 1, :], (N_HEADS, D_ATTN)), 0.0)
         for qq in range(n_new)], axis=0).astype(BF16)

    s_parts = []
    lf_parts = []
    for c in range(n_pages):
        s_parts.append(jnp.dot(qbd, page_t(k_refs[c]), preferred_element_type=F32))
        lf_parts.append(lf_refs[c][0])
    s_past = jnp.concatenate(s_parts, axis=1)
    ck_past = _lane_cumsum(jnp.concatenate(lf_parts, axis=1))
    n_past = ck_past.shape[1]
    c_total = ck_past[:, n_past - 1:n_past]

    kn = jnp.concatenate(
        [kn_ref[0], jnp.zeros((PAGE_SIZE - 8, D_ATTN), F32)], axis=0).astype(BF16)
    vn = jnp.concatenate(
        [vn_ref[0], jnp.zeros((PAGE_SIZE - 8, D_ATTN), F32)], axis=0).astype(BF16)
    s_new = lax.dot_general(qbd, kn, nt, preferred_element_type=F32)
    ck_new = c_total + _lane_cumsum(lfn_ref[0])

    tile = lambda a: jnp.concatenate([a] * n_new, axis=0)
    ck_past_r = tile(ck_past)
    ck_new_r = tile(ck_new)
    qq_row = lax.broadcasted_iota(jnp.int32, (nr, PAGE_SIZE), 0) // N_HEADS
    lane = lax.broadcasted_iota(jnp.int32, (nr, PAGE_SIZE), 1)
    cq = jnp.sum(jnp.where(lane == qq_row, ck_new_r, 0.0), axis=1, keepdims=True)

    lg_past = s_past + (cq - ck_past_r)
    lg_new = jnp.where(lane <= qq_row, s_new + (cq - ck_new_r), -jnp.inf)
    m = jnp.maximum(jnp.max(lg_past, axis=1, keepdims=True),
                    jnp.max(lg_new, axis=1, keepdims=True))
    p_past = jnp.exp(lg_past - m)
    p_new = jnp.exp(lg_new - m)
    denom = jnp.sum(p_past, axis=1, keepdims=True) + jnp.sum(p_new, axis=1, keepdims=True)
    acc = jnp.dot(p_new.astype(BF16), vn, preferred_element_type=F32)
    pb = p_past.astype(BF16)
    for c in range(n_pages):
        acc = acc + lax.dot_general(pb[:, c * PAGE_SIZE:(c + 1) * PAGE_SIZE], page_t(v_refs[c]),
                                    nt, preferred_element_type=F32)
    acc = acc / denom
    orow = lax.broadcasted_iota(jnp.int32, (8, D_ATTN), 0)
    out = jnp.zeros((8, D_ATTN), F32)
    for qq in range(n_new):
        blk = jnp.where(hmask, acc[qq * N_HEADS:(qq + 1) * N_HEADS, :], 0.0)
        out = jnp.where(orow == qq, jnp.sum(blk, axis=0, keepdims=True), out)
    o_ref[0] = out


def _sample_attn(page_table, q8, kn8, vn8, lfn_t, cache_kt, cache_vt, cache_lft, n_new):
    n_batch, n_pages = page_table.shape
    row_spec = lambda w: pl.BlockSpec((1, 8, w), lambda b, pt: (b, 0, 0))
    kv_page = lambda c: pl.BlockSpec((1, N_HEADS, HEAD_DIM, PAGE_SIZE),
                                     lambda b, pt: (pt[b * n_pages + c], 0, 0, 0))
    lf_page = lambda c: pl.BlockSpec((1, N_HEADS, PAGE_SIZE),
                                     lambda b, pt: (pt[b * n_pages + c], 0, 0))
    in_specs = [row_spec(D_ATTN), row_spec(D_ATTN), row_spec(D_ATTN),
                pl.BlockSpec((1, N_HEADS, LANES), lambda b, pt: (b, 0, 0))]
    in_specs += [kv_page(c) for c in range(n_pages)]
    in_specs += [kv_page(c) for c in range(n_pages)]
    in_specs += [lf_page(c) for c in range(n_pages)]
    grid_spec = pltpu.PrefetchScalarGridSpec(
        num_scalar_prefetch=1,
        grid=(n_batch,),
        in_specs=in_specs,
        out_specs=row_spec(D_ATTN),
    )
    return pl.pallas_call(
        functools.partial(_sample_attn_kernel, n_pages=n_pages, n_new=n_new),
        grid_spec=grid_spec,
        out_shape=jax.ShapeDtypeStruct((n_batch, 8, D_ATTN), F32),
        compiler_params=_params(1),
        name="sample_attn",
    )(page_table.reshape(-1), q8, kn8, vn8, lfn_t,
      *([cache_kt] * n_pages), *([cache_vt] * n_pages), *([cache_lft] * n_pages))


def _merge_kernel(o_ref, yb_ref, sga_ref, sgb_ref, x_ref, gt_ref, sh_ref, sc_ref,
                  gp1_ref, gp2_ref, woa_ref, wob_ref, wo_ref, x1_ref, h2_ref):
    o = jnp.concatenate([o_ref[hp] for hp in range(N_HP)], axis=1)
    a = jnp.dot(o, woa_ref[...], preferred_element_type=F32)
    b = jnp.dot(yb_ref[...], wob_ref[...], preferred_element_type=F32)
    mm = (sga_ref[...].astype(F32) * a + sgb_ref[...].astype(F32) * b).astype(BF16)
    m = jnp.dot(mm, wo_ref[...], preferred_element_type=F32)
    x1 = x_ref[...] + _mod(gt_ref) * _rms(m, gp1_ref[...])
    x1_ref[...] = x1
    h2 = _rms(x1, gp2_ref[...]) * (1.0 + _mod(sc_ref)) + _mod(sh_ref)
    h2_ref[...] = h2.astype(BF16)


def _merge(o_hp, yb, sga, sgb, x, mods, g_post1, g_pre2, w_oa, w_ob, w_o, *, tm,
           rows_per_batch, per_row):
    n = x.shape[0]
    if per_row:
        mod_spec = lambda c: pl.BlockSpec((tm, D_MODEL), lambda i: (i, c))
    else:
        bpb = rows_per_batch // tm
        mod_spec = lambda c: pl.BlockSpec((1, 1, D_MODEL), lambda i: (i // bpb, 0, c))
    row_spec = lambda w: pl.BlockSpec((tm, w), lambda i: (i, 0))
    const = lambda shape: pl.BlockSpec(shape, lambda i: (0,) * len(shape),
                                       pipeline_mode=pl.Buffered(1))
    return pl.pallas_call(
        _merge_kernel,
        grid=(n // tm,),
        in_specs=[pl.BlockSpec((N_HP, tm, LANES), lambda i: (0, i, 0)),
                  row_spec(D_CONV), row_spec(D_MODEL), row_spec(D_MODEL), row_spec(D_MODEL),
                  mod_spec(2), mod_spec(3), mod_spec(4),
                  const((1, D_MODEL)), const((1, D_MODEL)),
                  const((D_ATTN, D_MODEL)), const((D_CONV, D_MODEL)),
                  const((D_MODEL, D_MODEL))],
        out_specs=[row_spec(D_MODEL), row_spec(D_MODEL)],
        out_shape=[jax.ShapeDtypeStruct((n, D_MODEL), F32),
                   jax.ShapeDtypeStruct((n, D_MODEL), BF16)],
        compiler_params=_params(1),
        name="merge_sample" if per_row else "merge_prompt",
    )(o_hp, yb, sga, sgb, x, mods, mods, mods, g_post1, g_pre2, w_oa, w_ob, w_o)


def _ffn_kernel(h_ref, wg_ref, wu_ref, wout_ref, x1_ref, gt_ref, gp_ref, y_ref, acc_scr):
    j = pl.program_id(1)
    h = h_ref[...]
    g = jnp.dot(h, wg_ref[...], preferred_element_type=F32)
    u = jnp.dot(h, wu_ref[...], preferred_element_type=F32)
    act = (g * jax.nn.sigmoid(g) * u).astype(BF16)
    part = jnp.dot(act, wout_ref[...], preferred_element_type=F32)

    @pl.when(j == 0)
    def _():
        acc_scr[...] = part

    @pl.when(j > 0)
    def _():
        acc_scr[...] += part

    @pl.when(j == pl.num_programs(1) - 1)
    def _():
        y_ref[...] = x1_ref[...] + _mod(gt_ref) * _rms(acc_scr[...], gp_ref[...])


def _ffn(h2, x1, mods, g_post2, w_ffn_in, w_ffn_out, *, tm, tf, rows_per_batch, per_row):
    n = x1.shape[0]
    nf = D_FF // tf
    if per_row:
        mod_spec = pl.BlockSpec((tm, D_MODEL), lambda i, j: (i, 5))
    else:
        bpb = rows_per_batch // tm
        mod_spec = pl.BlockSpec((1, 1, D_MODEL), lambda i, j: (i // bpb, 0, 5))
    row_spec = pl.BlockSpec((tm, D_MODEL), lambda i, j: (i, 0))
    return pl.pallas_call(
        _ffn_kernel,
        grid=(n // tm, nf),
        in_specs=[row_spec,
                  pl.BlockSpec((D_MODEL, tf), lambda i, j: (0, j)),
                  pl.BlockSpec((D_MODEL, tf), lambda i, j: (0, j + nf)),
                  pl.BlockSpec((tf, D_MODEL), lambda i, j: (j, 0)),
                  row_spec, mod_spec,
                  pl.BlockSpec((1, D_MODEL), lambda i, j: (0, 0))],
        out_specs=row_spec,
        out_shape=jax.ShapeDtypeStruct((n, D_MODEL), F32),
        scratch_shapes=[pltpu.VMEM((tm, D_MODEL), F32)],
        compiler_params=_params(2),
        name="ffn_sample" if per_row else "ffn_prompt",
    )(h2, w_ffn_in, w_ffn_in, w_ffn_out, x1, mods, g_post2)


def _chunked_in_weights(w_in):
    offs = np.cumsum([0, D_ATTN, D_ATTN, D_ATTN, N_HEADS, D_CONV, D_CONV, D_CONV,
                      D_MODEL, D_MODEL])
    part = lambda idx: w_in[:, offs[idx]:offs[idx + 1]]
    q, k, v, f, u, bg, cg, ga, gb = [part(t) for t in range(9)]
    d = w_in.shape[0]
    ch = lambda a: a.reshape(d, N_CHUNK, -1)
    w_main = jnp.concatenate([ch(u), ch(cg), ch(bg), ch(ga), ch(gb), ch(q), ch(k), ch(v)],
                             axis=2).reshape(d, N_CHUNK * CHUNK_W).astype(BF16)
    w_f = jnp.pad(f, ((0, 0), (0, LANES - N_HEADS))).astype(BF16)
    return w_main, w_f


def kernel(x_prompt, x_sample, cache_k, cache_v, cache_logf, state_conv, page_table,
           c_prompt, c_sample, w_ada, b_ada, g_pre1, w_in, b_f, w_conv, w_oa, w_ob, w_o,
           g_post1, g_pre2, w_ffn_in, w_ffn_out, g_post2):
    depth = w_in.shape[0]
    assert depth == 1
    nb, t, d = x_prompt.shape
    db, ts, _ = x_sample.shape
    l = 0

    w_main, w_f = _chunked_in_weights(w_in[l])
    b_f_pad = jnp.pad(b_f[l], (0, LANES - N_HEADS)).reshape(1, LANES)
    w_oa_b = w_oa[l].astype(BF16)
    w_ob_b = w_ob[l].astype(BF16)
    w_o_b = w_o[l].astype(BF16)
    w_fin_b = w_ffn_in[l].astype(BF16)
    w_fout_b = w_ffn_out[l].astype(BF16)
    row2 = lambda a: a.reshape(1, -1)

    mods = _adaln(jnp.concatenate([c_prompt, c_sample], axis=0), w_ada[l], b_ada[l])
    mods_p = mods[:nb].reshape(nb, 1, 6 * d)
    mods_s = jnp.repeat(mods[nb:], ts, axis=0)

    xp = x_prompt.reshape(nb * t, d)
    (q_hp, kt, vt, k_hp, v_hp, zc, yb, sga, sgb, lft) = _in_proj(
        xp, row2(g_pre1[l]), mods_p, mods_p, w_main, w_f, b_f_pad, w_conv[l], None,
        tm=1024, rows_per_batch=t)
    cum, cumt = _cumsum(lft, nb, t)
    o_hp = _prompt_attn(q_hp, k_hp, v_hp, cum, cumt, nb, t, 512)
    x1, h2 = _merge(o_hp, yb, sga, sgb, xp, mods_p, row2(g_post1[l]), row2(g_pre2[l]),
                    w_oa_b, w_ob_b, w_o_b, tm=512, rows_per_batch=t, per_row=False)
    yp = _ffn(h2, x1, mods_p, row2(g_post2[l]), w_fin_b, w_fout_b, tm=512, tf=512,
              rows_per_batch=t, per_row=False)

    ns = db * ts
    xs = x_sample.reshape(ns, d)
    st = state_conv[l].astype(F32)
    state = (jnp.repeat(st[:, 0, :], ts, axis=0), jnp.repeat(st[:, 1, :], ts, axis=0))
    (qs_hp, kst, vst, _, _, zcs, ybs, sgas, sgbs, lfst) = _in_proj(
        xs, row2(g_pre1[l]), mods_s, mods_s, w_main, w_f, b_f_pad, w_conv[l], state,
        tm=ns, rows_per_batch=ts)
    ks_f = kst[0].T
    vs_f = vst[0].T
    lfs = lfst[0].reshape(N_HEADS, db, ts)
    pad8 = lambda a: jnp.pad(a.reshape(db, ts, -1), ((0, 0), (0, 8 - ts), (0, 0)))
    qs = qs_hp.transpose(1, 0, 2).reshape(ns, D_ATTN).astype(F32)
    lfn_t = jnp.pad(lfs.transpose(1, 0, 2), ((0, 0), (0, 0), (0, LANES - ts)))
    os8 = _sample_attn(page_table, pad8(qs), pad8(ks_f), pad8(vs_f), lfn_t,
                       cache_k[l].transpose(0, 2, 3, 1), cache_v[l].transpose(0, 2, 3, 1),
                       cache_logf[l].transpose(0, 2, 1), ts)
    os_hp = (os8[:, :ts, :].reshape(ns, N_HP, LANES).transpose(1, 0, 2).astype(BF16))
    x1s, h2s = _merge(os_hp, ybs, sgas, sgbs, xs, mods_s, row2(g_post1[l]), row2(g_pre2[l]),
                      w_oa_b, w_ob_b, w_o_b, tm=ns, rows_per_batch=ts, per_row=True)
    ys = _ffn(h2s, x1s, mods_s, row2(g_post2[l]), w_fin_b, w_fout_b, tm=ns, tf=512,
              rows_per_batch=ts, per_row=True)

    heads = lambda a, b_, t_: a.reshape(1, b_, t_, N_HEADS, HEAD_DIM)
    heads_t = lambda a: a.reshape(nb, N_HEADS, HEAD_DIM, t).transpose(0, 3, 1, 2)[None]
    return (yp.reshape(nb, t, d), ys.reshape(db, ts, d),
            heads_t(kt), heads_t(vt), lft.transpose(0, 2, 1)[None],
            zc.reshape(nb, t, D_CONV)[:, t - (CONV_W - 1):, :][None],
            heads(ks_f, db, ts), heads(vs_f, db, ts), lfs.transpose(1, 2, 0)[None],
            zcs.reshape(db, ts, D_CONV)[:, ts - (CONV_W - 1):, :][None])
```

```python
import functools

import jax
import jax.numpy as jnp
import numpy as np
from jax import lax
from jax.experimental import pallas as pl
from jax.experimental.pallas import tpu as pltpu

F32 = jnp.float32
BF16 = jnp.bfloat16

D_MODEL = 2048
N_HEADS = 16
HEAD_DIM = 64
D_ATTN = N_HEADS * HEAD_DIM
D_CONV = 1024
CONV_W = 3
D_FF = 5632
RMS_EPS = 1e-6
ATTN_SCALE = HEAD_DIM ** -0.5
LOG2E = 1.4426950408889634
PAGE_SIZE = 128

LANES = 128
N_HP = D_ATTN // LANES
N_CHUNK = 8
CHUNK_W = 6 * (D_ATTN // N_CHUNK) + 2 * (D_MODEL // N_CHUNK)
VMEM_LIMIT = 56 * 1024 * 1024


def _params(n_axes, vmem=VMEM_LIMIT):
    return pltpu.CompilerParams(
        dimension_semantics=("arbitrary",) * n_axes, vmem_limit_bytes=vmem)


def _rms(x, g):
    return x * lax.rsqrt(jnp.mean(x * x, axis=-1, keepdims=True) + RMS_EPS) * g


def _mod(ref):
    return ref[0] if len(ref.shape) == 3 else ref[...]


def _log_sigmoid(z):
    return jnp.minimum(z, 0.0) - jnp.log1p(jnp.exp(-jnp.abs(z)))


def _adaln_kernel(c_ref, w_ref, b_ref, o_ref):
    c = c_ref[...]
    s = (c * jax.nn.sigmoid(c)).astype(BF16)
    o_ref[...] = jnp.dot(s, w_ref[...].astype(BF16),
                         preferred_element_type=F32) + b_ref[...]


def _adaln(c, w_ada, b_ada):
    m, d = c.shape
    n = w_ada.shape[1]
    tn = 1024
    return pl.pallas_call(
        _adaln_kernel,
        grid=(n // tn,),
        in_specs=[pl.BlockSpec((m, d), lambda j: (0, 0)),
                  pl.BlockSpec((d, tn), lambda j: (0, j)),
                  pl.BlockSpec((1, tn), lambda j: (0, j))],
        out_specs=pl.BlockSpec((m, tn), lambda j: (0, j)),
        out_shape=jax.ShapeDtypeStruct((m, n), F32),
        compiler_params=_params(1),
        name="adaln",
    )(c, w_ada, b_ada.reshape(1, n))


def _in_proj_kernel(*refs, tm, blocks_per_batch, per_row, q_scale):
    if per_row:
        (x_ref, g_ref, sh_ref, sc_ref, w_ref, wf_ref, bf_ref, wc_ref, s0_ref, s1_ref,
         q_ref, kt_ref, vt_ref, kb_ref, vtb_ref, zc_ref, yb_ref, sga_ref, sgb_ref,
         lft_ref, h_scr) = refs
    else:
        (x_ref, g_ref, sh_ref, sc_ref, w_ref, wf_ref, bf_ref, wc_ref,
         q_ref, kt_ref, vt_ref, kb_ref, vtb_ref, zc_ref, yb_ref, sga_ref, sgb_ref,
         lft_ref, h_scr, halo_scr) = refs
    i = pl.program_id(0)
    j = pl.program_id(1)

    @pl.when(j == 0)
    def _():
        h = _rms(x_ref[...], g_ref[...]) * (1.0 + _mod(sc_ref)) + _mod(sh_ref)
        hb = h.astype(BF16)
        h_scr[...] = hb
        f = jnp.dot(hb, wf_ref[...], preferred_element_type=F32)
        lf = _log_sigmoid(f + bf_ref[...])
        lft_ref[0] = lf.T[0:N_HEADS, :]

    res = jnp.dot(h_scr[...], w_ref[...], preferred_element_type=F32)
    cw = D_ATTN // N_CHUNK
    gw = D_MODEL // N_CHUNK
    u = res[:, 0:cw]
    cg = res[:, cw:2 * cw]
    bg = res[:, 2 * cw:3 * cw]
    ga = res[:, 3 * cw:3 * cw + gw]
    gb = res[:, 3 * cw + gw:3 * cw + 2 * gw]
    o = 3 * cw + 2 * gw
    q = res[:, o:o + cw]
    k = res[:, o + cw:o + 2 * cw]
    v = res[:, o + 2 * cw:o + 3 * cw]

    q_ref[0] = (q * q_scale).astype(BF16)
    kt_ref[0] = k.T
    vt = v.T
    vt_ref[0] = vt
    vtb_ref[0] = vt.astype(BF16)
    kb_ref[0] = k.astype(BF16)
    sga_ref[...] = jax.nn.sigmoid(ga).astype(BF16)
    sgb_ref[...] = jax.nn.sigmoid(gb).astype(BF16)

    zc = cg * u
    zc_ref[...] = zc
    row = lax.broadcasted_iota(jnp.int32, zc.shape, 0)
    r1 = pltpu.roll(zc, 1, 0)
    r2 = pltpu.roll(zc, 2, 0)
    if per_row:
        t = row & 3
        s0 = s0_ref[...]
        s1 = s1_ref[...]
        p1 = jnp.where(t >= 1, r1, s1)
        p2 = jnp.where(t >= 2, r2, jnp.where(t == 1, s1, s0))
    else:
        first = (i % blocks_per_batch) == 0
        halo = halo_scr[j]
        h6 = jnp.where(first, 0.0, halo[6:7, :])
        h7 = jnp.where(first, 0.0, halo[7:8, :])
        p1 = jnp.where(row == 0, h7, r1)
        p2 = jnp.where(row == 0, h6, jnp.where(row == 1, h7, r2))
        halo_scr[j] = zc[tm - 8:tm, :]
    wc = wc_ref[...]
    y = p2 * wc[0:1, :]
    y = y + p1 * wc[1:2, :]
    y = y + zc * wc[2:3, :]
    yb_ref[...] = (bg * y).astype(BF16)


def _in_proj(x, g_pre, shift, scale, w_main, w_f, b_f, w_conv, state, *, tm, rows_per_batch):
    n = x.shape[0]
    per_row = state is not None
    nblk = n // tm
    cw = D_ATTN // N_CHUNK
    gw = D_MODEL // N_CHUNK
    if per_row:
        bpb = 1
        mod_spec = lambda c: pl.BlockSpec((tm, D_MODEL), lambda i, j: (i, c))
    else:
        bpb = rows_per_batch // tm
        mod_spec = lambda c: pl.BlockSpec((1, 1, D_MODEL), lambda i, j: (i // bpb, 0, c))
    in_specs = [
        pl.BlockSpec((tm, D_MODEL), lambda i, j: (i, 0)),
        pl.BlockSpec((1, D_MODEL), lambda i, j: (0, 0)),
        mod_spec(0), mod_spec(1),
        pl.BlockSpec((D_MODEL, CHUNK_W), lambda i, j: (0, j)),
        pl.BlockSpec((D_MODEL, LANES), lambda i, j: (0, 0)),
        pl.BlockSpec((1, LANES), lambda i, j: (0, 0)),
        pl.BlockSpec((CONV_W, cw), lambda i, j: (0, j)),
    ]
    args = [x, g_pre, shift, scale, w_main, w_f, b_f, w_conv]
    scratch = [pltpu.VMEM((tm, D_MODEL), BF16)]
    if per_row:
        in_specs += [pl.BlockSpec((tm, cw), lambda i, j: (i, j))] * 2
        args += [state[0], state[1]]
    else:
        scratch.append(pltpu.VMEM((N_CHUNK, 8, cw), F32))
    hp_spec = pl.BlockSpec((1, tm, LANES), lambda i, j: (j, i, 0))
    col_spec = lambda w: pl.BlockSpec((tm, w), lambda i, j: (i, j))
    nbat = nblk // bpb
    t_spec = lambda rows: pl.BlockSpec(
        (1, rows, tm), lambda i, j: (i // bpb, j if rows == LANES else 0, i % bpb))
    t_shape = lambda rows, dt=F32: jax.ShapeDtypeStruct((nbat, rows, bpb * tm), dt)
    out_specs = [hp_spec, t_spec(LANES), t_spec(LANES), hp_spec, t_spec(LANES),
                 col_spec(cw), col_spec(cw), col_spec(gw), col_spec(gw),
                 t_spec(N_HEADS)]
    hp_shape = jax.ShapeDtypeStruct((N_HP, n, LANES), BF16)
    out_shape = [hp_shape, t_shape(D_ATTN), t_shape(D_ATTN), hp_shape, t_shape(D_ATTN, BF16),
                 jax.ShapeDtypeStruct((n, D_CONV), F32), jax.ShapeDtypeStruct((n, D_CONV), BF16),
                 jax.ShapeDtypeStruct((n, D_MODEL), BF16), jax.ShapeDtypeStruct((n, D_MODEL), BF16),
                 t_shape(N_HEADS)]
    return pl.pallas_call(
        functools.partial(_in_proj_kernel, tm=tm, blocks_per_batch=bpb, per_row=per_row,
                          q_scale=ATTN_SCALE if per_row else ATTN_SCALE * LOG2E),
        grid=(nblk, N_CHUNK),
        in_specs=in_specs, out_specs=out_specs, out_shape=out_shape,
        scratch_shapes=scratch,
        compiler_params=_params(2),
        name="in_proj_sample" if per_row else "in_proj_prompt",
    )(*args)


def _lane_cumsum(x):
    n = x.shape[-1]
    lane = lax.broadcasted_iota(jnp.int32, x.shape, x.ndim - 1)
    s = 1
    while s < n:
        x = x + jnp.where(lane >= s, pltpu.roll(x, s, x.ndim - 1), 0.0)
        s *= 2
    return x


def _cumsum_kernel(lft_ref, cumt_ref):
    cumt_ref[0] = _lane_cumsum(lft_ref[0]) * LOG2E


def _cumsum(lft, n_batch, t):
    spec = pl.BlockSpec((1, N_HEADS, t), lambda b: (b, 0, 0))
    return pl.pallas_call(
        _cumsum_kernel,
        grid=(n_batch,),
        in_specs=[spec], out_specs=spec,
        out_shape=jax.ShapeDtypeStruct((n_batch, N_HEADS, t), F32),
        compiler_params=_params(1),
        name="cumsum",
    )(lft)


ACC_ROWS = HEAD_DIM + 16
Q_CHUNK = 512


def _prompt_attn_kernel(qi_ref, kj_ref, q_ref, k_ref, vt_ref, cq_ref, ck_ref, o_ref,
                        acc_scr, m_scr, *, tq):
    t = pl.program_id(1)
    qi = qi_ref[t]
    kj = kj_ref[t]
    tk = tq

    @pl.when(kj == 0)
    def _():
        acc_scr[...] = jnp.zeros_like(acc_scr)
        m_scr[...] = jnp.full_like(m_scr, -jnp.inf)

    lane = lax.broadcasted_iota(jnp.int32, (1, LANES), 1)
    lo_half = lane < HEAD_DIM
    ones_rows = jnp.ones((ACC_ROWS - HEAD_DIM, tk), BF16)
    row8 = lax.broadcasted_iota(jnp.int32, (8, tq), 0)

    def head_pair(hp, carry, *, masked):
        q2 = q_ref[hp]
        k2 = k_ref[hp]
        m_pair = m_scr[hp]
        acc_pair = acc_scr[hp]
        cq_pair = [cq_ref[0, pl.ds(2 * hp + e, 1), :] for e in range(2)]
        ck_pair = [ck_ref[0, pl.ds(2 * hp + e, 1), :] for e in range(2)]
        vt_pair = vt_ref[0, pl.ds(pl.multiple_of(hp * LANES, LANES), LANES), :]
        m_out = []
        acc_out = []
        for e in range(2):
            sel = lo_half if e == 0 else jnp.logical_not(lo_half)
            km = jnp.where(sel, k2, jnp.zeros_like(k2))
            ck_col = jnp.broadcast_to(ck_pair[e], (LANES, tk)).T
            vaug = jnp.concatenate(
                [vt_pair[e * HEAD_DIM:(e + 1) * HEAD_DIM, :], ones_rows], axis=0)
            m_cols = []
            acc_cols = []
            for c in range(tq // Q_CHUNK):
                cols = slice(c * Q_CHUNK, (c + 1) * Q_CHUNK)
                st = lax.dot_general(km, q2[cols, :], (((1,), (1,)), ((), ())),
                                     preferred_element_type=F32)
                lg = st - pltpu.repeat(ck_col, Q_CHUNK // LANES, axis=1)
                if masked:
                    key = lax.broadcasted_iota(jnp.int32, lg.shape, 0)
                    qry = lax.broadcasted_iota(jnp.int32, lg.shape, 1) + c * Q_CHUNK
                    lg = jnp.where(key <= qry, lg, -jnp.inf)
                cq = cq_pair[e][:, cols]
                m_prev = m_pair[e:e + 1, cols]
                m_new = jnp.maximum(m_prev, jnp.max(lg, axis=0, keepdims=True) + cq)
                p = jnp.exp2(lg + (cq - m_new)).astype(BF16)
                alpha = jnp.exp2(m_prev - m_new)
                acc_cols.append(acc_pair[e * ACC_ROWS:(e + 1) * ACC_ROWS, cols] * alpha
                                + jnp.dot(vaug, p, preferred_element_type=F32))
                m_cols.append(m_new)
            acc_out.append(jnp.concatenate(acc_cols, axis=1))
            m_out.append(jnp.concatenate(m_cols, axis=1))
        m_scr[hp] = jnp.where(row8 == 0, m_out[0], m_out[1])
        acc_scr[hp] = jnp.concatenate(acc_out, axis=0)
        return carry

    @pl.when(kj < qi)
    def _():
        lax.fori_loop(0, N_HP, functools.partial(head_pair, masked=False), 0, unroll=2)

    @pl.when(kj == qi)
    def _():
        lax.fori_loop(0, N_HP, functools.partial(head_pair, masked=True), 0, unroll=2)

        def fin(hp, carry):
            acc_pair = acc_scr[hp]
            halves = []
            for e in range(2):
                a = acc_pair[e * ACC_ROWS:(e + 1) * ACC_ROWS, :]
                halves.append(a[0:HEAD_DIM, :] * (1.0 / a[HEAD_DIM:HEAD_DIM + 1, :]))
            o_ref[hp] = jnp.concatenate(halves, axis=0).T.astype(BF16)
            return carry
        lax.fori_loop(0, N_HP, fin, 0)


def _prompt_attn(q_hp, k_hp, vt_b, cumt, n_batch, t, tq):
    nq = t // tq
    pairs = [(a, b) for a in range(nq) for b in range(a + 1)]
    qi_tab = jnp.asarray([p[0] for p in pairs], jnp.int32)
    kj_tab = jnp.asarray([p[1] for p in pairs], jnp.int32)
    blk = (N_HP, tq, LANES)
    grid_spec = pltpu.PrefetchScalarGridSpec(
        num_scalar_prefetch=2,
        grid=(n_batch, len(pairs)),
        in_specs=[
            pl.BlockSpec(blk, lambda b, s, qi, kj: (0, b * nq + qi[s], 0)),
            pl.BlockSpec(blk, lambda b, s, qi, kj: (0, b * nq + kj[s], 0)),
            pl.BlockSpec((1, D_ATTN, tq), lambda b, s, qi, kj: (b, 0, kj[s])),
            pl.BlockSpec((1, N_HEADS, tq), lambda b, s, qi, kj: (b, 0, qi[s])),
            pl.BlockSpec((1, N_HEADS, tq), lambda b, s, qi, kj: (b, 0, kj[s])),
        ],
        out_specs=pl.BlockSpec(blk, lambda b, s, qi, kj: (0, b * nq + qi[s], 0)),
        scratch_shapes=[pltpu.VMEM((N_HP, 2 * ACC_ROWS, tq), F32),
                        pltpu.VMEM((N_HP, 8, tq), F32)],
    )
    return pl.pallas_call(
        functools.partial(_prompt_attn_kernel, tq=tq),
        grid_spec=grid_spec,
        out_shape=jax.ShapeDtypeStruct((N_HP, n_batch * t, LANES), BF16),
        compiler_params=_params(2),
        name="prompt_attn",
    )(qi_tab, kj_tab, q_hp, k_hp, vt_b, cumt, cumt)


def _sample_attn_kernel(*refs, n_pages, n_new):
    pt_ref = refs[0]
    q_ref, kn_ref, vn_ref, lfn_ref = refs[1:5]
    k_refs = refs[5:5 + n_pages]
    v_refs = refs[5 + n_pages:5 + 2 * n_pages]
    lf_refs = refs[5 + 2 * n_pages:5 + 3 * n_pages]
    o_ref = refs[5 + 3 * n_pages]
    del pt_ref
    nr = n_new * N_HEADS
    nt = ((1,), (1,)), ((), ())
    page_t = lambda ref: ref[0].reshape(D_ATTN, PAGE_SIZE).astype(BF16)

    q = q_ref[0]
    hrow = lax.broadcasted_iota(jnp.int32, (N_HEADS, D_ATTN), 0)
    hlane = lax.broadcasted_iota(jnp.int32, (N_HEADS, D_ATTN), 1) // HEAD_DIM
    hmask = hrow == hlane
    qbd = jnp.concatenate(
        [jnp.where(hmask, jnp.broadcast_to(q[qq:qq + 1, :], (N_HEADS, D_ATTN)), 0.0)
         for qq in range(n_new)], axis=0).astype(BF16)

    s_parts = []
    lf_parts = []
    for c in range(n_pages):
        s_parts.append(jnp.dot(qbd, page_t(k_refs[c]), preferred_element_type=F32))
        lf_parts.append(lf_refs[c][0])
    s_past = jnp.concatenate(s_parts, axis=1)
    ck_past = _lane_cumsum(jnp.concatenate(lf_parts, axis=1))
    n_past = ck_past.shape[1]
    c_total = ck_past[:, n_past - 1:n_past]

    kn = jnp.concatenate(
        [kn_ref[0], jnp.zeros((PAGE_SIZE - 8, D_ATTN), F32)], axis=0).astype(BF16)
    vn = jnp.concatenate(
        [vn_ref[0], jnp.zeros((PAGE_SIZE - 8, D_ATTN), F32)], axis=0).astype(BF16)
    s_new = lax.dot_general(qbd, kn, nt, preferred_element_type=F32)
    ck_new = c_total + _lane_cumsum(lfn_ref[0])

    tile = lambda a: jnp.concatenate([a] * n_new, axis=0)
    ck_past_r = tile(ck_past)
    ck_new_r = tile(ck_new)
    qq_row = lax.broadcasted_iota(jnp.int32, (nr, PAGE_SIZE), 0) // N_HEADS
    lane = lax.broadcasted_iota(jnp.int32, (nr, PAGE_SIZE), 1)
    cq = jnp.sum(jnp.where(lane == qq_row, ck_new_r, 0.0), axis=1, keepdims=True)

    lg_past = s_past + (cq - ck_past_r)
    lg_new = jnp.where(lane <= qq_row, s_new + (cq - ck_new_r), -jnp.inf)
    m = jnp.maximum(jnp.max(lg_past, axis=1, keepdims=True),
                    jnp.max(lg_new, axis=1, keepdims=True))
    p_past = jnp.exp(lg_past - m)
    p_new = jnp.exp(lg_new - m)
    denom = jnp.sum(p_past, axis=1, keepdims=True) + jnp.sum(p_new, axis=1, keepdims=True)
    acc = jnp.dot(p_new.astype(BF16), vn, preferred_element_type=F32)
    pb = p_past.astype(BF16)
    for c in range(n_pages):
        acc = acc + lax.dot_general(pb[:, c * PAGE_SIZE:(c + 1) * PAGE_SIZE], page_t(v_refs[c]),
                                    nt, preferred_element_type=F32)
    acc = acc / denom
    orow = lax.broadcasted_iota(jnp.int32, (8, D_ATTN), 0)
    out = jnp.zeros((8, D_ATTN), F32)
    for qq in range(n_new):
        blk = jnp.where(hmask, acc[qq * N_HEADS:(qq + 1) * N_HEADS, :], 0.0)
        out = jnp.where(orow == qq, jnp.sum(blk, axis=0, keepdims=True), out)
    o_ref[0] = out


def _sample_attn(page_table, q8, kn8, vn8, lfn_t, cache_kt, cache_vt, cache_lft, n_new):
    n_batch, n_pages = page_table.shape
    row_spec = lambda w: pl.BlockSpec((1, 8, w), lambda b, pt: (b, 0, 0))
    kv_page = lambda c: pl.BlockSpec((1, N_HEADS, HEAD_DIM, PAGE_SIZE),
                                     lambda b, pt: (pt[b * n_pages + c], 0, 0, 0))
    lf_page = lambda c: pl.BlockSpec((1, N_HEADS, PAGE_SIZE),
                                     lambda b, pt: (pt[b * n_pages + c], 0, 0))
    in_specs = [row_spec(D_ATTN), row_spec(D_ATTN), row_spec(D_ATTN),
                pl.BlockSpec((1, N_HEADS, LANES), lambda b, pt: (b, 0, 0))]
    in_specs += [kv_page(c) for c in range(n_pages)]
    in_specs += [kv_page(c) for c in range(n_pages)]
    in_specs += [lf_page(c) for c in range(n_pages)]
    grid_spec = pltpu.PrefetchScalarGridSpec(
        num_scalar_prefetch=1,
        grid=(n_batch,),
        in_specs=in_specs,
        out_specs=row_spec(D_ATTN),
    )
    return pl.pallas_call(
        functools.partial(_sample_attn_kernel, n_pages=n_pages, n_new=n_new),
        grid_spec=grid_spec,
        out_shape=jax.ShapeDtypeStruct((n_batch, 8, D_ATTN), F32),
        compiler_params=_params(1),
        name="sample_attn",
    )(page_table.reshape(-1), q8, kn8, vn8, lfn_t,
      *([cache_kt] * n_pages), *([cache_vt] * n_pages), *([cache_lft] * n_pages))


def _merge_kernel(o_ref, yb_ref, sga_ref, sgb_ref, x_ref, gt_ref, sh_ref, sc_ref,
                  gp1_ref, gp2_ref, woa_ref, wob_ref, wo_ref, x1_ref, h2_ref):
    o = jnp.concatenate([o_ref[hp] for hp in range(N_HP)], axis=1)
    a = jnp.dot(o, woa_ref[...], preferred_element_type=F32)
    b = jnp.dot(yb_ref[...], wob_ref[...], preferred_element_type=F32)
    mm = (sga_ref[...].astype(F32) * a + sgb_ref[...].astype(F32) * b).astype(BF16)
    m = jnp.dot(mm, wo_ref[...], preferred_element_type=F32)
    x1 = x_ref[...] + _mod(gt_ref) * _rms(m, gp1_ref[...])
    x1_ref[...] = x1
    h2 = _rms(x1, gp2_ref[...]) * (1.0 + _mod(sc_ref)) + _mod(sh_ref)
    h2_ref[...] = h2.astype(BF16)


def _merge(o_hp, yb, sga, sgb, x, mods, g_post1, g_pre2, w_oa, w_ob, w_o, *, tm,
           rows_per_batch, per_row):
    n = x.shape[0]
    if per_row:
        mod_spec = lambda c: pl.BlockSpec((tm, D_MODEL), lambda i: (i, c))
    else:
        bpb = rows_per_batch // tm
        mod_spec = lambda c: pl.BlockSpec((1, 1, D_MODEL), lambda i: (i // bpb, 0, c))
    row_spec = lambda w: pl.BlockSpec((tm, w), lambda i: (i, 0))
    const = lambda shape: pl.BlockSpec(shape, lambda i: (0,) * len(shape),
                                       pipeline_mode=pl.Buffered(1))
    return pl.pallas_call(
        _merge_kernel,
        grid=(n // tm,),
        in_specs=[pl.BlockSpec((N_HP, tm, LANES), lambda i: (0, i, 0)),
                  row_spec(D_CONV), row_spec(D_MODEL), row_spec(D_MODEL), row_spec(D_MODEL),
                  mod_spec(2), mod_spec(3), mod_spec(4),
                  const((1, D_MODEL)), const((1, D_MODEL)),
                  const((D_ATTN, D_MODEL)), const((D_CONV, D_MODEL)),
                  const((D_MODEL, D_MODEL))],
        out_specs=[row_spec(D_MODEL), row_spec(D_MODEL)],
        out_shape=[jax.ShapeDtypeStruct((n, D_MODEL), F32),
                   jax.ShapeDtypeStruct((n, D_MODEL), BF16)],
        compiler_params=_params(1),
        name="merge_sample" if per_row else "merge_prompt",
    )(o_hp, yb, sga, sgb, x, mods, mods, mods, g_post1, g_pre2, w_oa, w_ob, w_o)


def _ffn_kernel(h_ref, wg_ref, wu_ref, wout_ref, x1_ref, gt_ref, gp_ref, y_ref, acc_scr):
    j = pl.program_id(1)

    @pl.when(j == 0)
    def _():
        acc_scr[...] = jnp.zeros_like(acc_scr)

    h = h_ref[...]
    g = jnp.dot(h, wg_ref[...], preferred_element_type=F32)
    u = jnp.dot(h, wu_ref[...], preferred_element_type=F32)
    act = (g * jax.nn.sigmoid(g) * u).astype(BF16)
    acc_scr[...] += jnp.dot(act, wout_ref[...], preferred_element_type=F32)

    @pl.when(j == pl.num_programs(1) - 1)
    def _():
        y_ref[...] = x1_ref[...] + _mod(gt_ref) * _rms(acc_scr[...], gp_ref[...])


def _ffn(h2, x1, mods, g_post2, w_ffn_in, w_ffn_out, *, tm, tf, rows_per_batch, per_row):
    n = x1.shape[0]
    nf = D_FF // tf
    if per_row:
        mod_spec = pl.BlockSpec((tm, D_MODEL), lambda i, j: (i, 5))
    else:
        bpb = rows_per_batch // tm
        mod_spec = pl.BlockSpec((1, 1, D_MODEL), lambda i, j: (i // bpb, 0, 5))
    row_spec = pl.BlockSpec((tm, D_MODEL), lambda i, j: (i, 0))
    return pl.pallas_call(
        _ffn_kernel,
        grid=(n // tm, nf),
        in_specs=[row_spec,
                  pl.BlockSpec((D_MODEL, tf), lambda i, j: (0, j)),
                  pl.BlockSpec((D_MODEL, tf), lambda i, j: (0, j + nf)),
                  pl.BlockSpec((tf, D_MODEL), lambda i, j: (j, 0)),
                  row_spec, mod_spec,
                  pl.BlockSpec((1, D_MODEL), lambda i, j: (0, 0))],
        out_specs=row_spec,
        out_shape=jax.ShapeDtypeStruct((n, D_MODEL), F32),
        scratch_shapes=[pltpu.VMEM((tm, D_MODEL), F32)],
        compiler_params=_params(2),
        name="ffn_sample" if per_row else "ffn_prompt",
    )(h2, w_ffn_in, w_ffn_in, w_ffn_out, x1, mods, g_post2)


def _chunked_in_weights(w_in):
    offs = np.cumsum([0, D_ATTN, D_ATTN, D_ATTN, N_HEADS, D_CONV, D_CONV, D_CONV,
                      D_MODEL, D_MODEL])
    part = lambda idx: w_in[:, offs[idx]:offs[idx + 1]]
    q, k, v, f, u, bg, cg, ga, gb = [part(t) for t in range(9)]
    d = w_in.shape[0]
    ch = lambda a: a.reshape(d, N_CHUNK, -1)
    w_main = jnp.concatenate([ch(u), ch(cg), ch(bg), ch(ga), ch(gb), ch(q), ch(k), ch(v)],
                             axis=2).reshape(d, N_CHUNK * CHUNK_W).astype(BF16)
    w_f = jnp.pad(f, ((0, 0), (0, LANES - N_HEADS))).astype(BF16)
    return w_main, w_f


def kernel(x_prompt, x_sample, cache_k, cache_v, cache_logf, state_conv, page_table,
           c_prompt, c_sample, w_ada, b_ada, g_pre1, w_in, b_f, w_conv, w_oa, w_ob, w_o,
           g_post1, g_pre2, w_ffn_in, w_ffn_out, g_post2):
    depth = w_in.shape[0]
    assert depth == 1
    nb, t, d = x_prompt.shape
    db, ts, _ = x_sample.shape
    l = 0

    w_main, w_f = _chunked_in_weights(w_in[l])
    b_f_pad = jnp.pad(b_f[l], (0, LANES - N_HEADS)).reshape(1, LANES)
    w_oa_b = w_oa[l].astype(BF16)
    w_ob_b = w_ob[l].astype(BF16)
    w_o_b = w_o[l].astype(BF16)
    w_fin_b = w_ffn_in[l].astype(BF16)
    w_fout_b = w_ffn_out[l].astype(BF16)
    row2 = lambda a: a.reshape(1, -1)

    mods = _adaln(jnp.concatenate([c_prompt, c_sample], axis=0), w_ada[l], b_ada[l])
    mods_p = mods[:nb].reshape(nb, 1, 6 * d)
    mods_s = jnp.repeat(mods[nb:], ts, axis=0)

    xp = x_prompt.reshape(nb * t, d)
    (q_hp, kt, vt, k_hp, vt_b, zc, yb, sga, sgb, lft) = _in_proj(
        xp, row2(g_pre1[l]), mods_p, mods_p, w_main, w_f, b_f_pad, w_conv[l], None,
        tm=1024, rows_per_batch=t)
    cumt = _cumsum(lft, nb, t)
    o_hp = _prompt_attn(q_hp, k_hp, vt_b, cumt, nb, t, 512)
    x1, h2 = _merge(o_hp, yb, sga, sgb, xp, mods_p, row2(g_post1[l]), row2(g_pre2[l]),
                    w_oa_b, w_ob_b, w_o_b, tm=512, rows_per_batch=t, per_row=False)
    yp = _ffn(h2, x1, mods_p, row2(g_post2[l]), w_fin_b, w_fout_b, tm=512, tf=512,
              rows_per_batch=t, per_row=False)

    ns = db * ts
    xs = x_sample.reshape(ns, d)
    st = state_conv[l].astype(F32)
    state = (jnp.repeat(st[:, 0, :], ts, axis=0), jnp.repeat(st[:, 1, :], ts, axis=0))
    (qs_hp, kst, vst, _, _, zcs, ybs, sgas, sgbs, lfst) = _in_proj(
        xs, row2(g_pre1[l]), mods_s, mods_s, w_main, w_f, b_f_pad, w_conv[l], state,
        tm=ns, rows_per_batch=ts)
    ks_f = kst[0].T
    vs_f = vst[0].T
    lfs = lfst[0].reshape(N_HEADS, db, ts)
    pad8 = lambda a: jnp.pad(a.reshape(db, ts, -1), ((0, 0), (0, 8 - ts), (0, 0)))
    qs = qs_hp.transpose(1, 0, 2).reshape(ns, D_ATTN).astype(F32)
    lfn_t = jnp.pad(lfs.transpose(1, 0, 2), ((0, 0), (0, 0), (0, LANES - ts)))
    os8 = _sample_attn(page_table, pad8(qs), pad8(ks_f), pad8(vs_f), lfn_t,
                       cache_k[l].transpose(0, 2, 3, 1), cache_v[l].transpose(0, 2, 3, 1),
                       cache_logf[l].transpose(0, 2, 1), ts)
    os_hp = (os8[:, :ts, :].reshape(ns, N_HP, LANES).transpose(1, 0, 2).astype(BF16))
    x1s, h2s = _merge(os_hp, ybs, sgas, sgbs, xs, mods_s, row2(g_post1[l]), row2(g_pre2[l]),
                      w_oa_b, w_ob_b, w_o_b, tm=ns, rows_per_batch=ts, per_row=True)
    ys = _ffn(h2s, x1s, mods_s, row2(g_post2[l]), w_fin_b, w_fout_b, tm=ns, tf=512,
              rows_per_batch=ts, per_row=True)

    heads = lambda a, b_, t_: a.reshape(1, b_, t_, N_HEADS, HEAD_DIM)
    heads_t = lambda a: a.reshape(nb, N_HEADS, HEAD_DIM, t).transpose(0, 3, 1, 2)[None]
    return (yp.reshape(nb, t, d), ys.reshape(db, ts, d),
            heads_t(kt), heads_t(vt), lft.transpose(0, 2, 1)[None],
            zc.reshape(nb, t, D_CONV)[:, t - (CONV_W - 1):, :][None],
            heads(ks_f, db, ts), heads(vs_f, db, ts), lfs.transpose(1, 2, 0)[None],
            zcs.reshape(db, ts, D_CONV)[:, ts - (CONV_W - 1):, :][None])
```

```python
import functools

import jax
import jax.numpy as jnp
import numpy as np
from jax import lax
from jax.experimental import pallas as pl
from jax.experimental.pallas import tpu as pltpu

F32 = jnp.float32
BF16 = jnp.bfloat16

D_MODEL = 2048
N_HEADS = 16
HEAD_DIM = 64
D_ATTN = N_HEADS * HEAD_DIM
D_CONV = 1024
CONV_W = 3
D_FF = 5632
RMS_EPS = 1e-6
ATTN_SCALE = HEAD_DIM ** -0.5
LOG2E = 1.4426950408889634
PAGE_SIZE = 128

LANES = 128
N_HP = D_ATTN // LANES
N_CHUNK = 8
CHUNK_W = 6 * (D_ATTN // N_CHUNK) + 2 * (D_MODEL // N_CHUNK)
VMEM_LIMIT = 56 * 1024 * 1024


def _params(n_axes, vmem=VMEM_LIMIT, flags=None):
    return pltpu.CompilerParams(
        dimension_semantics=("arbitrary",) * n_axes, vmem_limit_bytes=vmem, flags=flags)


def _rms(x, g):
    return x * lax.rsqrt(jnp.mean(x * x, axis=-1, keepdims=True) + RMS_EPS) * g


def _repeat_rows(m, rep):
    n = m.shape[0]
    assert rep & (rep - 1) == 0
    src = lax.broadcasted_iota(jnp.int32, (n * rep, n), 0) >> (rep.bit_length() - 1)
    col = lax.broadcasted_iota(jnp.int32, (n * rep, n), 1)
    onehot = jnp.where(src == col, 1.0, 0.0).astype(BF16)
    out = None
    for _ in range(3):
        piece = m.astype(BF16)
        part = jnp.dot(onehot, piece, preferred_element_type=F32)
        out = part if out is None else out + part
        m = m - piece.astype(F32)
    return out


def _mod(ref, rep=1):
    if len(ref.shape) == 3:
        return ref[0]
    return ref[...] if rep == 1 else _repeat_rows(ref[...], rep)


def _log_sigmoid(z):
    return jnp.minimum(z, 0.0) - jnp.log1p(jnp.exp(-jnp.abs(z)))


def _adaln_kernel(c_ref, w_ref, b_ref, o_ref):
    c = c_ref[...]
    s = (c * jax.nn.sigmoid(c)).astype(BF16)
    o_ref[...] = jnp.dot(s, w_ref[...].astype(BF16),
                         preferred_element_type=F32) + b_ref[...]


def _adaln(c, w_ada, b_ada):
    m, d = c.shape
    n = w_ada.shape[1]
    tn = 512
    return pl.pallas_call(
        _adaln_kernel,
        grid=(n // tn,),
        in_specs=[pl.BlockSpec((m, d), lambda j: (0, 0)),
                  pl.BlockSpec((d, tn), lambda j: (0, j)),
                  pl.BlockSpec((1, tn), lambda j: (0, j))],
        out_specs=pl.BlockSpec((m, tn), lambda j: (0, j)),
        out_shape=jax.ShapeDtypeStruct((m, n), F32),
        compiler_params=_params(1),
        name="adaln",
    )(c, w_ada, b_ada.reshape(1, n))


def _in_proj_kernel(*refs, tm, blocks_per_batch, per_row, rep, q_scale):
    if per_row:
        (x_ref, g_ref, sh_ref, sc_ref, w_ref, wf_ref, bf_ref, wc_ref, s0_ref, s1_ref,
         q_ref, kt_ref, vt_ref, kb_ref, vtb_ref, zc_ref, yb_ref, sga_ref, sgb_ref,
         lft_ref, h_scr) = refs
    else:
        (x_ref, g_ref, sh_ref, sc_ref, w_ref, wf_ref, bf_ref, wc_ref,
         q_ref, kt_ref, vt_ref, kb_ref, vtb_ref, zc_ref, yb_ref, sga_ref, sgb_ref,
         lft_ref, h_scr, halo_scr) = refs
    i = pl.program_id(0)
    j = pl.program_id(1)

    @pl.when(j == 0)
    def _():
        h = _rms(x_ref[...], g_ref[...]) * (1.0 + _mod(sc_ref, rep)) + _mod(sh_ref, rep)
        hb = h.astype(BF16)
        h_scr[...] = hb
        f = jnp.dot(hb, wf_ref[...], preferred_element_type=F32)
        lf = _log_sigmoid(f + bf_ref[...])
        lft_ref[0] = lf.T[0:N_HEADS, :]

    res = jnp.dot(h_scr[...], w_ref[...], preferred_element_type=F32)
    cw = D_ATTN // N_CHUNK
    gw = D_MODEL // N_CHUNK
    u = res[:, 0:cw]
    cg = res[:, cw:2 * cw]
    bg = res[:, 2 * cw:3 * cw]
    ga = res[:, 3 * cw:3 * cw + gw]
    gb = res[:, 3 * cw + gw:3 * cw + 2 * gw]
    o = 3 * cw + 2 * gw
    q = res[:, o:o + cw]
    k = res[:, o + cw:o + 2 * cw]
    v = res[:, o + 2 * cw:o + 3 * cw]

    q_ref[0] = (q * q_scale).astype(BF16)
    kt_ref[0] = k.T
    vt = v.T
    vt_ref[0] = vt
    vtb_ref[0] = vt.astype(BF16)
    kb_ref[0] = k.astype(BF16)
    sga_ref[...] = jax.nn.sigmoid(ga).astype(BF16)
    sgb_ref[...] = jax.nn.sigmoid(gb).astype(BF16)

    zc = cg * u
    zc_ref[...] = zc
    row = lax.broadcasted_iota(jnp.int32, zc.shape, 0)
    r1 = pltpu.roll(zc, 1, 0)
    r2 = pltpu.roll(zc, 2, 0)
    if per_row:
        t = row & 3
        s0 = s0_ref[...]
        s1 = s1_ref[...]
        p1 = jnp.where(t >= 1, r1, s1)
        p2 = jnp.where(t >= 2, r2, jnp.where(t == 1, s1, s0))
    else:
        first = (i % blocks_per_batch) == 0
        halo = halo_scr[j]
        h6 = jnp.where(first, 0.0, halo[6:7, :])
        h7 = jnp.where(first, 0.0, halo[7:8, :])
        p1 = jnp.where(row == 0, h7, r1)
        p2 = jnp.where(row == 0, h6, jnp.where(row == 1, h7, r2))
        halo_scr[j] = zc[tm - 8:tm, :]
    wc = wc_ref[...]
    y = p2 * wc[0:1, :]
    y = y + p1 * wc[1:2, :]
    y = y + zc * wc[2:3, :]
    yb_ref[...] = (bg * y).astype(BF16)


def _in_proj(x, g_pre, shift, scale, w_main, w_f, b_f, w_conv, state, *, tm, rows_per_batch):
    n = x.shape[0]
    per_row = state is not None
    nblk = n // tm
    cw = D_ATTN // N_CHUNK
    gw = D_MODEL // N_CHUNK
    if per_row:
        bpb = 1
        mod_spec = lambda c: pl.BlockSpec((tm // rows_per_batch, D_MODEL), lambda i, j: (i, c))
    else:
        bpb = rows_per_batch // tm
        mod_spec = lambda c: pl.BlockSpec((1, 1, D_MODEL), lambda i, j: (i // bpb, 0, c))
    in_specs = [
        pl.BlockSpec((tm, D_MODEL), lambda i, j: (i, 0)),
        pl.BlockSpec((1, D_MODEL), lambda i, j: (0, 0)),
        mod_spec(0), mod_spec(1),
        pl.BlockSpec((D_MODEL, CHUNK_W), lambda i, j: (0, j)),
        pl.BlockSpec((D_MODEL, LANES), lambda i, j: (0, 0)),
        pl.BlockSpec((1, LANES), lambda i, j: (0, 0)),
        pl.BlockSpec((CONV_W, cw), lambda i, j: (0, j)),
    ]
    args = [x, g_pre, shift, scale, w_main, w_f, b_f, w_conv]
    scratch = [pltpu.VMEM((tm, D_MODEL), BF16)]
    if per_row:
        in_specs += [pl.BlockSpec((tm, cw), lambda i, j: (i, j))] * 2
        args += [state[0], state[1]]
    else:
        scratch.append(pltpu.VMEM((N_CHUNK, 8, cw), F32))
    hp_spec = pl.BlockSpec((1, tm, LANES), lambda i, j: (j, i, 0))
    col_spec = lambda w: pl.BlockSpec((tm, w), lambda i, j: (i, j))
    nbat = nblk // bpb
    t_spec = lambda rows: pl.BlockSpec(
        (1, rows, tm), lambda i, j: (i // bpb, j if rows == LANES else 0, i % bpb))
    t_shape = lambda rows, dt=F32: jax.ShapeDtypeStruct((nbat, rows, bpb * tm), dt)
    out_specs = [hp_spec, t_spec(LANES), t_spec(LANES), hp_spec, t_spec(LANES),
                 col_spec(cw), col_spec(cw), col_spec(gw), col_spec(gw),
                 t_spec(N_HEADS)]
    hp_shape = jax.ShapeDtypeStruct((N_HP, n, LANES), BF16)
    out_shape = [hp_shape, t_shape(D_ATTN), t_shape(D_ATTN), hp_shape, t_shape(D_ATTN, BF16),
                 jax.ShapeDtypeStruct((n, D_CONV), F32), jax.ShapeDtypeStruct((n, D_CONV), BF16),
                 jax.ShapeDtypeStruct((n, D_MODEL), BF16), jax.ShapeDtypeStruct((n, D_MODEL), BF16),
                 t_shape(N_HEADS)]
    return pl.pallas_call(
        functools.partial(_in_proj_kernel, tm=tm, blocks_per_batch=bpb, per_row=per_row,
                          rep=rows_per_batch if per_row else 1,
                          q_scale=ATTN_SCALE if per_row else ATTN_SCALE * LOG2E),
        grid=(nblk, N_CHUNK),
        in_specs=in_specs, out_specs=out_specs, out_shape=out_shape,
        scratch_shapes=scratch,
        compiler_params=_params(2),
        name="in_proj_sample" if per_row else "in_proj_prompt",
    )(*args)


def _lane_cumsum(x):
    n = x.shape[-1]
    lane = lax.broadcasted_iota(jnp.int32, x.shape, x.ndim - 1)
    s = 1
    while s < n:
        x = x + jnp.where(lane >= s, pltpu.roll(x, s, x.ndim - 1), 0.0)
        s *= 2
    return x


def _cumsum_kernel(lft_ref, cumt_ref):
    cumt_ref[0] = _lane_cumsum(lft_ref[0]) * LOG2E


def _cumsum(lft, n_batch, t):
    spec = pl.BlockSpec((1, N_HEADS, t), lambda b: (b, 0, 0))
    return pl.pallas_call(
        _cumsum_kernel,
        grid=(n_batch,),
        in_specs=[spec], out_specs=spec,
        out_shape=jax.ShapeDtypeStruct((n_batch, N_HEADS, t), F32),
        compiler_params=_params(1),
        name="cumsum",
    )(lft)


ACC_ROWS = HEAD_DIM + 16


def _prompt_attn_kernel(qi_ref, kj_ref, q_ref, k_ref, vt_ref, cq_ref, ck_ref, o_ref,
                        acc_scr, m_scr, lg_scr, st_scr, *, tq, tk):
    t = pl.program_id(1)
    qi = qi_ref[t]
    kj = kj_ref[t]
    key0 = kj * tk - qi * tq

    @pl.when(kj == 0)
    def _():
        acc_scr[...] = jnp.zeros_like(acc_scr)
        m_scr[...] = jnp.full_like(m_scr, -jnp.inf)

    lane = lax.broadcasted_iota(jnp.int32, (1, LANES), 1)
    lo_half = lane < HEAD_DIM
    ones_rows = jnp.ones((ACC_ROWS - HEAD_DIM, tk), BF16)
    row8 = lax.broadcasted_iota(jnp.int32, (8, tq), 0)

    def logits(hp, slot, masked):
        q2 = q_ref[hp]
        k2 = k_ref[hp]
        m_pair = m_scr[hp]
        m_out = []
        stats = []
        for e in range(2):
            head = 2 * hp + e
            sel = lo_half if e == 0 else jnp.logical_not(lo_half)
            km = jnp.where(sel, k2, jnp.zeros_like(k2))
            st = lax.dot_general(km, q2, (((1,), (1,)), ((), ())),
                                 preferred_element_type=F32)
            ck_col = jnp.broadcast_to(ck_ref[0, head:head + 1, :], (LANES, tk)).T
            lg = st - pltpu.repeat(ck_col, tq // LANES, axis=1)
            if masked:
                key = lax.broadcasted_iota(jnp.int32, lg.shape, 0) + key0
                qry = lax.broadcasted_iota(jnp.int32, lg.shape, 1)
                lg = jnp.where(key <= qry, lg, -jnp.inf)
            cq = cq_ref[0, head:head + 1, :]
            m_prev = m_pair[e:e + 1, :]
            m_new = jnp.maximum(m_prev, jnp.max(lg, axis=0, keepdims=True) + cq)
            lg_scr[slot, e] = lg
            stats += [cq - m_new, jnp.exp2(m_prev - m_new)]
            m_out.append(m_new)
        st_scr[slot] = jnp.where(row8 == 0, stats[0], jnp.where(
            row8 == 1, stats[1], jnp.where(row8 == 2, stats[2], stats[3])))
        m_scr[hp] = jnp.where(row8 == 0, m_out[0], m_out[1])

    def update(hp, slot):
        vt_pair = vt_ref[0, hp * LANES:(hp + 1) * LANES, :]
        acc_pair = acc_scr[hp]
        stat = st_scr[slot]
        acc_out = []
        for e in range(2):
            p = jnp.exp2(lg_scr[slot, e] + stat[2 * e:2 * e + 1, :]).astype(BF16)
            vaug = jnp.concatenate(
                [vt_pair[e * HEAD_DIM:(e + 1) * HEAD_DIM, :], ones_rows], axis=0)
            acc_out.append(acc_pair[e * ACC_ROWS:(e + 1) * ACC_ROWS, :]
                           * stat[2 * e + 1:2 * e + 2, :]
                           + jnp.dot(vaug, p, preferred_element_type=F32))
        acc_scr[hp] = jnp.concatenate(acc_out, axis=0)

    def all_pairs(masked):
        logits(0, 0, masked)
        for hp in range(N_HP):
            update(hp, hp % 2)
            if hp + 1 < N_HP:
                logits(hp + 1, (hp + 1) % 2, masked)

    @pl.when(key0 + tk <= 0)
    def _():
        all_pairs(False)

    @pl.when(key0 + tk > 0)
    def _():
        all_pairs(True)

    @pl.when(key0 + tk == tq)
    def _():
        def fin(hp, carry):
            acc_pair = acc_scr[hp]
            halves = []
            for e in range(2):
                a = acc_pair[e * ACC_ROWS:(e + 1) * ACC_ROWS, :]
                halves.append(a[0:HEAD_DIM, :] * (1.0 / a[HEAD_DIM:HEAD_DIM + 1, :]))
            o_ref[hp] = jnp.concatenate(halves, axis=0).T.astype(BF16)
            return carry
        lax.fori_loop(0, N_HP, fin, 0)


def _prompt_attn(q_hp, k_hp, vt_b, cumt, n_batch, t, tq, tk):
    nq = t // tq
    nk = t // tk
    pairs = [(a, b) for a in range(nq) for b in range((a + 1) * tq // tk)]
    qi_tab = jnp.asarray([p[0] for p in pairs], jnp.int32)
    kj_tab = jnp.asarray([p[1] for p in pairs], jnp.int32)
    qblk = (N_HP, tq, LANES)
    grid_spec = pltpu.PrefetchScalarGridSpec(
        num_scalar_prefetch=2,
        grid=(n_batch, len(pairs)),
        in_specs=[
            pl.BlockSpec(qblk, lambda b, s, qi, kj: (0, b * nq + qi[s], 0)),
            pl.BlockSpec((N_HP, tk, LANES), lambda b, s, qi, kj: (0, b * nk + kj[s], 0)),
            pl.BlockSpec((1, D_ATTN, tk), lambda b, s, qi, kj: (b, 0, kj[s])),
            pl.BlockSpec((1, N_HEADS, tq), lambda b, s, qi, kj: (b, 0, qi[s])),
            pl.BlockSpec((1, N_HEADS, tk), lambda b, s, qi, kj: (b, 0, kj[s])),
        ],
        out_specs=pl.BlockSpec(qblk, lambda b, s, qi, kj: (0, b * nq + qi[s], 0)),
        scratch_shapes=[pltpu.VMEM((N_HP, 2 * ACC_ROWS, tq), F32),
                        pltpu.VMEM((N_HP, 8, tq), F32),
                        pltpu.VMEM((2, 2, tk, tq), F32),
                        pltpu.VMEM((2, 8, tq), F32)],
    )
    return pl.pallas_call(
        functools.partial(_prompt_attn_kernel, tq=tq, tk=tk),
        grid_spec=grid_spec,
        out_shape=jax.ShapeDtypeStruct((N_HP, n_batch * t, LANES), BF16),
        compiler_params=_params(2),
        name="prompt_attn",
    )(qi_tab, kj_tab, q_hp, k_hp, vt_b, cumt, cumt)


def _sample_attn_kernel(*refs, n_pages, n_new):
    pt_ref = refs[0]
    q_ref, kn_ref, vn_ref, lfn_ref = refs[1:5]
    k_refs = refs[5:5 + n_pages]
    v_refs = refs[5 + n_pages:5 + 2 * n_pages]
    lf_refs = refs[5 + 2 * n_pages:5 + 3 * n_pages]
    o_ref = refs[5 + 3 * n_pages]
    del pt_ref
    nr = n_new * N_HEADS
    nt = ((1,), (1,)), ((), ())
    page_t = lambda ref: ref[0].reshape(D_ATTN, PAGE_SIZE).astype(BF16)

    q = q_ref[0]
    hrow = lax.broadcasted_iota(jnp.int32, (N_HEADS, D_ATTN), 0)
    hlane = lax.broadcasted_iota(jnp.int32, (N_HEADS, D_ATTN), 1) // HEAD_DIM
    hmask = hrow == hlane
    qbd = jnp.concatenate(
        [jnp.where(hmask, jnp.broadcast_to(q[qq:qq + 1, :], (N_HEADS, D_ATTN)), 0.0)
         for qq in range(n_new)], axis=0).astype(BF16)

    s_parts = []
    lf_parts = []
    for c in range(n_pages):
        s_parts.append(jnp.dot(qbd, page_t(k_refs[c]), preferred_element_type=F32))
        lf_parts.append(lf_refs[c][0])
    s_past = jnp.concatenate(s_parts, axis=1)
    ck_past = _lane_cumsum(jnp.concatenate(lf_parts, axis=1))
    n_past = ck_past.shape[1]
    c_total = ck_past[:, n_past - 1:n_past]

    kn = jnp.concatenate(
        [kn_ref[0], jnp.zeros((PAGE_SIZE - 8, D_ATTN), F32)], axis=0).astype(BF16)
    vn = jnp.concatenate(
        [vn_ref[0], jnp.zeros((PAGE_SIZE - 8, D_ATTN), F32)], axis=0).astype(BF16)
    s_new = lax.dot_general(qbd, kn, nt, preferred_element_type=F32)
    ck_new = c_total + _lane_cumsum(lfn_ref[0])

    tile = lambda a: jnp.concatenate([a] * n_new, axis=0)
    ck_past_r = tile(ck_past)
    ck_new_r = tile(ck_new)
    qq_row = lax.broadcasted_iota(jnp.int32, (nr, PAGE_SIZE), 0) // N_HEADS
    lane = lax.broadcasted_iota(jnp.int32, (nr, PAGE_SIZE), 1)
    cq = jnp.sum(jnp.where(lane == qq_row, ck_new_r, 0.0), axis=1, keepdims=True)

    lg_past = s_past + (cq - ck_past_r)
    lg_new = jnp.where(lane <= qq_row, s_new + (cq - ck_new_r), -jnp.inf)
    m = jnp.maximum(jnp.max(lg_past, axis=1, keepdims=True),
                    jnp.max(lg_new, axis=1, keepdims=True))
    p_past = jnp.exp(lg_past - m)
    p_new = jnp.exp(lg_new - m)
    denom = jnp.sum(p_past, axis=1, keepdims=True) + jnp.sum(p_new, axis=1, keepdims=True)
    acc = jnp.dot(p_new.astype(BF16), vn, preferred_element_type=F32)
    pb = p_past.astype(BF16)
    for c in range(n_pages):
        acc = acc + lax.dot_general(pb[:, c * PAGE_SIZE:(c + 1) * PAGE_SIZE], page_t(v_refs[c]),
                                    nt, preferred_element_type=F32)
    acc = acc / denom
    orow = lax.broadcasted_iota(jnp.int32, (8, D_ATTN), 0)
    out = jnp.zeros((8, D_ATTN), F32)
    for qq in range(n_new):
        blk = jnp.where(hmask, acc[qq * N_HEADS:(qq + 1) * N_HEADS, :], 0.0)
        out = jnp.where(orow == qq, jnp.sum(blk, axis=0, keepdims=True), out)
    o_ref[0] = out


def _sample_attn(page_table, q8, kn8, vn8, lfn_t, cache_kt, cache_vt, cache_lft, n_new):
    n_batch, n_pages = page_table.shape
    row_spec = lambda w: pl.BlockSpec((1, 8, w), lambda b, pt: (b, 0, 0))
    kv_page = lambda c: pl.BlockSpec((1, N_HEADS, HEAD_DIM, PAGE_SIZE),
                                     lambda b, pt: (pt[b * n_pages + c], 0, 0, 0))
    lf_page = lambda c: pl.BlockSpec((1, N_HEADS, PAGE_SIZE),
                                     lambda b, pt: (pt[b * n_pages + c], 0, 0))
    in_specs = [row_spec(D_ATTN), row_spec(D_ATTN), row_spec(D_ATTN),
                pl.BlockSpec((1, N_HEADS, LANES), lambda b, pt: (b, 0, 0))]
    in_specs += [kv_page(c) for c in range(n_pages)]
    in_specs += [kv_page(c) for c in range(n_pages)]
    in_specs += [lf_page(c) for c in range(n_pages)]
    grid_spec = pltpu.PrefetchScalarGridSpec(
        num_scalar_prefetch=1,
        grid=(n_batch,),
        in_specs=in_specs,
        out_specs=row_spec(D_ATTN),
    )
    return pl.pallas_call(
        functools.partial(_sample_attn_kernel, n_pages=n_pages, n_new=n_new),
        grid_spec=grid_spec,
        out_shape=jax.ShapeDtypeStruct((n_batch, 8, D_ATTN), F32),
        compiler_params=_params(1),
        name="sample_attn",
    )(page_table.reshape(-1), q8, kn8, vn8, lfn_t,
      *([cache_kt] * n_pages), *([cache_vt] * n_pages), *([cache_lft] * n_pages))


def _merge_kernel(o_ref, yb_ref, sga_ref, sgb_ref, x_ref, gt_ref, sh_ref, sc_ref,
                  gp1_ref, gp2_ref, woa_ref, wob_ref, wo_ref, x1_ref, h2_ref, *, rep):
    o = jnp.concatenate([o_ref[hp] for hp in range(N_HP)], axis=1)
    a = jnp.dot(o, woa_ref[...], preferred_element_type=F32)
    b = jnp.dot(yb_ref[...], wob_ref[...], preferred_element_type=F32)
    mm = (sga_ref[...].astype(F32) * a + sgb_ref[...].astype(F32) * b).astype(BF16)
    m = jnp.dot(mm, wo_ref[...], preferred_element_type=F32)
    x1 = x_ref[...] + _mod(gt_ref, rep) * _rms(m, gp1_ref[...])
    x1_ref[...] = x1
    h2 = _rms(x1, gp2_ref[...]) * (1.0 + _mod(sc_ref, rep)) + _mod(sh_ref, rep)
    h2_ref[...] = h2.astype(BF16)


def _merge(o_hp, yb, sga, sgb, x, mods, g_post1, g_pre2, w_oa, w_ob, w_o, *, tm,
           rows_per_batch, per_row):
    n = x.shape[0]
    if per_row:
        mod_spec = lambda c: pl.BlockSpec((tm // rows_per_batch, D_MODEL), lambda i: (i, c))
    else:
        bpb = rows_per_batch // tm
        mod_spec = lambda c: pl.BlockSpec((1, 1, D_MODEL), lambda i: (i // bpb, 0, c))
    row_spec = lambda w: pl.BlockSpec((tm, w), lambda i: (i, 0))
    const = lambda shape: pl.BlockSpec(shape, lambda i: (0,) * len(shape),
                                       pipeline_mode=pl.Buffered(1))
    return pl.pallas_call(
        functools.partial(_merge_kernel, rep=rows_per_batch if per_row else 1),
        grid=(n // tm,),
        in_specs=[pl.BlockSpec((N_HP, tm, LANES), lambda i: (0, i, 0)),
                  row_spec(D_CONV), row_spec(D_MODEL), row_spec(D_MODEL), row_spec(D_MODEL),
                  mod_spec(2), mod_spec(3), mod_spec(4),
                  const((1, D_MODEL)), const((1, D_MODEL)),
                  const((D_ATTN, D_MODEL)), const((D_CONV, D_MODEL)),
                  const((D_MODEL, D_MODEL))],
        out_specs=[row_spec(D_MODEL), row_spec(D_MODEL)],
        out_shape=[jax.ShapeDtypeStruct((n, D_MODEL), F32),
                   jax.ShapeDtypeStruct((n, D_MODEL), BF16)],
        compiler_params=_params(1),
        name="merge_sample" if per_row else "merge_prompt",
    )(o_hp, yb, sga, sgb, x, mods, mods, mods, g_post1, g_pre2, w_oa, w_ob, w_o)


def _ffn_kernel(h_ref, wg_ref, wu_ref, wout_ref, x1_ref, gt_ref, gp_ref, y_ref, acc_scr,
                *, rep):
    j = pl.program_id(1)

    @pl.when(j == 0)
    def _():
        acc_scr[...] = jnp.zeros_like(acc_scr)

    h = h_ref[...]
    g = jnp.dot(h, wg_ref[...], preferred_element_type=F32)
    u = jnp.dot(h, wu_ref[...], preferred_element_type=F32)
    act = (g * jax.nn.sigmoid(g) * u).astype(BF16)
    acc_scr[...] += jnp.dot(act, wout_ref[...], preferred_element_type=F32)

    @pl.when(j == pl.num_programs(1) - 1)
    def _():
        y_ref[...] = x1_ref[...] + _mod(gt_ref, rep) * _rms(acc_scr[...], gp_ref[...])


def _ffn(h2, x1, mods, g_post2, w_ffn_in, w_ffn_out, *, tm, tf, rows_per_batch, per_row):
    n = x1.shape[0]
    nf = D_FF // tf
    if per_row:
        mod_spec = pl.BlockSpec((tm // rows_per_batch, D_MODEL), lambda i, j: (i, 5))
    else:
        bpb = rows_per_batch // tm
        mod_spec = pl.BlockSpec((1, 1, D_MODEL), lambda i, j: (i // bpb, 0, 5))
    row_spec = pl.BlockSpec((tm, D_MODEL), lambda i, j: (i, 0))
    return pl.pallas_call(
        functools.partial(_ffn_kernel, rep=rows_per_batch if per_row else 1),
        grid=(n // tm, nf),
        in_specs=[row_spec,
                  pl.BlockSpec((D_MODEL, tf), lambda i, j: (0, j)),
                  pl.BlockSpec((D_MODEL, tf), lambda i, j: (0, j + nf)),
                  pl.BlockSpec((tf, D_MODEL), lambda i, j: (j, 0)),
                  row_spec, mod_spec,
                  pl.BlockSpec((1, D_MODEL), lambda i, j: (0, 0))],
        out_specs=row_spec,
        out_shape=jax.ShapeDtypeStruct((n, D_MODEL), F32),
        scratch_shapes=[pltpu.VMEM((tm, D_MODEL), F32)],
        compiler_params=_params(2),
        name="ffn_sample" if per_row else "ffn_prompt",
    )(h2, w_ffn_in, w_ffn_in, w_ffn_out, x1, mods, g_post2)


def _chunked_in_weights(w_in):
    offs = np.cumsum([0, D_ATTN, D_ATTN, D_ATTN, N_HEADS, D_CONV, D_CONV, D_CONV,
                      D_MODEL, D_MODEL])
    part = lambda idx: w_in[:, offs[idx]:offs[idx + 1]]
    q, k, v, f, u, bg, cg, ga, gb = [part(t) for t in range(9)]
    d = w_in.shape[0]
    ch = lambda a: a.reshape(d, N_CHUNK, -1)
    w_main = jnp.concatenate([ch(u), ch(cg), ch(bg), ch(ga), ch(gb), ch(q), ch(k), ch(v)],
                             axis=2).reshape(d, N_CHUNK * CHUNK_W).astype(BF16)
    w_f = jnp.pad(f, ((0, 0), (0, LANES - N_HEADS))).astype(BF16)
    return w_main, w_f


def kernel(x_prompt, x_sample, cache_k, cache_v, cache_logf, state_conv, page_table,
           c_prompt, c_sample, w_ada, b_ada, g_pre1, w_in, b_f, w_conv, w_oa, w_ob, w_o,
           g_post1, g_pre2, w_ffn_in, w_ffn_out, g_post2):
    depth = w_in.shape[0]
    assert depth == 1
    nb, t, d = x_prompt.shape
    db, ts, _ = x_sample.shape
    l = 0

    w_main, w_f = _chunked_in_weights(w_in[l])
    b_f_pad = jnp.pad(b_f[l], (0, LANES - N_HEADS)).reshape(1, LANES)
    w_oa_b = w_oa[l].astype(BF16)
    w_ob_b = w_ob[l].astype(BF16)
    w_o_b = w_o[l].astype(BF16)
    w_fin_b = w_ffn_in[l].astype(BF16)
    w_fout_b = w_ffn_out[l].astype(BF16)
    row2 = lambda a: a.reshape(1, -1)

    mods = _adaln(jnp.concatenate([c_prompt, c_sample], axis=0), w_ada[l], b_ada[l])
    mods_p = mods[:nb].reshape(nb, 1, 6 * d)
    mods_s = mods[nb:]

    xp = x_prompt.reshape(nb * t, d)
    (q_hp, kt, vt, k_hp, vt_b, zc, yb, sga, sgb, lft) = _in_proj(
        xp, row2(g_pre1[l]), mods_p, mods_p, w_main, w_f, b_f_pad, w_conv[l], None,
        tm=1024, rows_per_batch=t)
    cumt = _cumsum(lft, nb, t)
    o_hp = _prompt_attn(q_hp, k_hp, vt_b, cumt, nb, t, 512, 512)
    x1, h2 = _merge(o_hp, yb, sga, sgb, xp, mods_p, row2(g_post1[l]), row2(g_pre2[l]),
                    w_oa_b, w_ob_b, w_o_b, tm=512, rows_per_batch=t, per_row=False)
    yp = _ffn(h2, x1, mods_p, row2(g_post2[l]), w_fin_b, w_fout_b, tm=512, tf=512,
              rows_per_batch=t, per_row=False)

    ns = db * ts
    xs = x_sample.reshape(ns, d)
    st = state_conv[l].astype(F32)
    state = (jnp.repeat(st[:, 0, :], ts, axis=0), jnp.repeat(st[:, 1, :], ts, axis=0))
    (qs_hp, kst, vst, _, _, zcs, ybs, sgas, sgbs, lfst) = _in_proj(
        xs, row2(g_pre1[l]), mods_s, mods_s, w_main, w_f, b_f_pad, w_conv[l], state,
        tm=ns, rows_per_batch=ts)
    ks_f = kst[0].T
    vs_f = vst[0].T
    lfs = lfst[0].reshape(N_HEADS, db, ts)
    pad8 = lambda a: jnp.pad(a.reshape(db, ts, -1), ((0, 0), (0, 8 - ts), (0, 0)))
    qs = qs_hp.transpose(1, 0, 2).reshape(ns, D_ATTN).astype(F32)
    lfn_t = jnp.pad(lfs.transpose(1, 0, 2), ((0, 0), (0, 0), (0, LANES - ts)))
    os8 = _sample_attn(page_table, pad8(qs), pad8(ks_f), pad8(vs_f), lfn_t,
                       cache_k[l].transpose(0, 2, 3, 1), cache_v[l].transpose(0, 2, 3, 1),
                       cache_logf[l].transpose(0, 2, 1), ts)
    os_hp = (os8[:, :ts, :].reshape(ns, N_HP, LANES).transpose(1, 0, 2).astype(BF16))
    x1s, h2s = _merge(os_hp, ybs, sgas, sgbs, xs, mods_s, row2(g_post1[l]), row2(g_pre2[l]),
                      w_oa_b, w_ob_b, w_o_b, tm=ns, rows_per_batch=ts, per_row=True)
    ys = _ffn(h2s, x1s, mods_s, row2(g_post2[l]), w_fin_b, w_fout_b, tm=ns, tf=512,
              rows_per_batch=ts, per_row=True)

    heads = lambda a, b_, t_: a.reshape(1, b_, t_, N_HEADS, HEAD_DIM)
    heads_t = lambda a: a.reshape(nb, N_HEADS, HEAD_DIM, t).transpose(0, 3, 1, 2)[None]
    return (yp.reshape(nb, t, d), ys.reshape(db, ts, d),
            heads_t(kt), heads_t(vt), lft.transpose(0, 2, 1)[None],
            zc.reshape(nb, t, D_CONV)[:, t - (CONV_W - 1):, :][None],
            heads(ks_f, db, ts), heads(vs_f, db, ts), lfs.transpose(1, 2, 0)[None],
            zcs.reshape(db, ts, D_CONV)[:, ts - (CONV_W - 1):, :][None])
```

```python
import functools

import jax
import jax.numpy as jnp
import numpy as np
from jax import lax
from jax.experimental import pallas as pl
from jax.experimental.pallas import tpu as pltpu

F32 = jnp.float32
BF16 = jnp.bfloat16

D_MODEL = 2048
N_HEADS = 16
HEAD_DIM = 64
D_ATTN = N_HEADS * HEAD_DIM
D_CONV = 1024
CONV_W = 3
D_FF = 5632
RMS_EPS = 1e-6
ATTN_SCALE = HEAD_DIM ** -0.5
LOG2E = 1.4426950408889634
PAGE_SIZE = 128

LANES = 128
N_HP = D_ATTN // LANES
N_CHUNK = 8
CHUNK_W = 6 * (D_ATTN // N_CHUNK) + 2 * (D_MODEL // N_CHUNK)
VMEM_LIMIT = 56 * 1024 * 1024


def _params(n_axes, vmem=VMEM_LIMIT, flags=None):
    return pltpu.CompilerParams(
        dimension_semantics=("arbitrary",) * n_axes, vmem_limit_bytes=vmem, flags=flags)


def _rms(x, g):
    return x * lax.rsqrt(jnp.mean(x * x, axis=-1, keepdims=True) + RMS_EPS) * g


def _repeat_rows(m, rep):
    n = m.shape[0]
    assert rep & (rep - 1) == 0
    src = lax.broadcasted_iota(jnp.int32, (n * rep, n), 0) >> (rep.bit_length() - 1)
    col = lax.broadcasted_iota(jnp.int32, (n * rep, n), 1)
    onehot = jnp.where(src == col, 1.0, 0.0).astype(BF16)
    out = None
    for _ in range(3):
        piece = m.astype(BF16)
        part = jnp.dot(onehot, piece, preferred_element_type=F32)
        out = part if out is None else out + part
        m = m - piece.astype(F32)
    return out


def _mod(ref, rep=1):
    if len(ref.shape) == 3:
        return ref[0]
    return ref[...] if rep == 1 else _repeat_rows(ref[...], rep)


def _log_sigmoid(z):
    return jnp.minimum(z, 0.0) - jnp.log1p(jnp.exp(-jnp.abs(z)))


def _adaln_kernel(c_ref, w_ref, b_ref, o_ref):
    @pl.when(pl.program_id(0) == 0)
    def _():
        o_ref[...] = jnp.broadcast_to(b_ref[...], o_ref.shape)

    c = c_ref[...]
    s = (c * jax.nn.sigmoid(c)).astype(BF16)
    o_ref[...] += jnp.dot(s, w_ref[...].astype(BF16), preferred_element_type=F32)


def _adaln(c, w_ada, b_ada):
    m, d = c.shape
    n = w_ada.shape[1]
    tk = LANES
    return pl.pallas_call(
        _adaln_kernel,
        grid=(d // tk,),
        in_specs=[pl.BlockSpec((m, tk), lambda k: (0, k)),
                  pl.BlockSpec((tk, n), lambda k: (k, 0)),
                  pl.BlockSpec((1, n), lambda k: (0, 0))],
        out_specs=pl.BlockSpec((m, n), lambda k: (0, 0)),
        out_shape=jax.ShapeDtypeStruct((m, n), F32),
        compiler_params=_params(1),
        name="adaln",
    )(c, w_ada, b_ada.reshape(1, n))


def _in_proj_kernel(*refs, tm, blocks_per_batch, per_row, rep, q_scale):
    if per_row:
        (x_ref, g_ref, sh_ref, sc_ref, w_ref, wf_ref, bf_ref, wc_ref, s0_ref, s1_ref,
         q_ref, kt_ref, vt_ref, kb_ref, vtb_ref, zc_ref, yb_ref, sga_ref, sgb_ref,
         lft_ref, h_scr) = refs
    else:
        (x_ref, g_ref, sh_ref, sc_ref, w_ref, wf_ref, bf_ref, wc_ref,
         q_ref, kt_ref, vt_ref, kb_ref, vtb_ref, zc_ref, yb_ref, sga_ref, sgb_ref,
         lft_ref, h_scr, halo_scr) = refs
    i = pl.program_id(0)
    j = pl.program_id(1)

    @pl.when(j == 0)
    def _():
        h = _rms(x_ref[...], g_ref[...]) * (1.0 + _mod(sc_ref, rep)) + _mod(sh_ref, rep)
        hb = h.astype(BF16)
        h_scr[...] = hb
        f = jnp.dot(hb, wf_ref[...], preferred_element_type=F32)
        lf = _log_sigmoid(f + bf_ref[...])
        lft_ref[0] = lf.T[0:N_HEADS, :]

    res = jnp.dot(h_scr[...], w_ref[...], preferred_element_type=F32)
    cw = D_ATTN // N_CHUNK
    gw = D_MODEL // N_CHUNK
    u = res[:, 0:cw]
    cg = res[:, cw:2 * cw]
    bg = res[:, 2 * cw:3 * cw]
    ga = res[:, 3 * cw:3 * cw + gw]
    gb = res[:, 3 * cw + gw:3 * cw + 2 * gw]
    o = 3 * cw + 2 * gw
    q = res[:, o:o + cw]
    k = res[:, o + cw:o + 2 * cw]
    v = res[:, o + 2 * cw:o + 3 * cw]

    q_ref[0] = (q * q_scale).astype(BF16)
    kt_ref[0] = k.T
    vt = v.T
    vt_ref[0] = vt
    vtb_ref[0] = vt.astype(BF16)
    kb_ref[0] = k.astype(BF16)
    sga_ref[...] = jax.nn.sigmoid(ga).astype(BF16)
    sgb_ref[...] = jax.nn.sigmoid(gb).astype(BF16)

    zc = cg * u
    zc_ref[...] = zc
    row = lax.broadcasted_iota(jnp.int32, zc.shape, 0)
    r1 = pltpu.roll(zc, 1, 0)
    r2 = pltpu.roll(zc, 2, 0)
    if per_row:
        t = row & 3
        s0 = s0_ref[...]
        s1 = s1_ref[...]
        p1 = jnp.where(t >= 1, r1, s1)
        p2 = jnp.where(t >= 2, r2, jnp.where(t == 1, s1, s0))
    else:
        first = (i % blocks_per_batch) == 0
        halo = halo_scr[j]
        h6 = jnp.where(first, 0.0, halo[6:7, :])
        h7 = jnp.where(first, 0.0, halo[7:8, :])
        p1 = jnp.where(row == 0, h7, r1)
        p2 = jnp.where(row == 0, h6, jnp.where(row == 1, h7, r2))
        halo_scr[j] = zc[tm - 8:tm, :]
    wc = wc_ref[...]
    y = p2 * wc[0:1, :]
    y = y + p1 * wc[1:2, :]
    y = y + zc * wc[2:3, :]
    yb_ref[...] = (bg * y).astype(BF16)


def _in_proj(x, g_pre, shift, scale, w_main, w_f, b_f, w_conv, state, *, tm, rows_per_batch):
    n = x.shape[0]
    per_row = state is not None
    nblk = n // tm
    cw = D_ATTN // N_CHUNK
    gw = D_MODEL // N_CHUNK
    if per_row:
        bpb = 1
        mod_spec = lambda c: pl.BlockSpec((tm // rows_per_batch, D_MODEL), lambda i, j: (i, c))
    else:
        bpb = rows_per_batch // tm
        mod_spec = lambda c: pl.BlockSpec((1, 1, D_MODEL), lambda i, j: (i // bpb, 0, c))
    in_specs = [
        pl.BlockSpec((tm, D_MODEL), lambda i, j: (i, 0)),
        pl.BlockSpec((1, D_MODEL), lambda i, j: (0, 0)),
        mod_spec(0), mod_spec(1),
        pl.BlockSpec((D_MODEL, CHUNK_W), lambda i, j: (0, j)),
        pl.BlockSpec((D_MODEL, LANES), lambda i, j: (0, 0)),
        pl.BlockSpec((1, LANES), lambda i, j: (0, 0)),
        pl.BlockSpec((CONV_W, cw), lambda i, j: (0, j)),
    ]
    args = [x, g_pre, shift, scale, w_main, w_f, b_f, w_conv]
    scratch = [pltpu.VMEM((tm, D_MODEL), BF16)]
    if per_row:
        in_specs += [pl.BlockSpec((tm, cw), lambda i, j: (i, j))] * 2
        args += [state[0], state[1]]
    else:
        scratch.append(pltpu.VMEM((N_CHUNK, 8, cw), F32))
    hp_spec = pl.BlockSpec((1, tm, LANES), lambda i, j: (j, i, 0))
    col_spec = lambda w: pl.BlockSpec((tm, w), lambda i, j: (i, j))
    nbat = nblk // bpb
    t_spec = lambda rows: pl.BlockSpec(
        (1, rows, tm), lambda i, j: (i // bpb, j if rows == LANES else 0, i % bpb))
    t_shape = lambda rows, dt=F32: jax.ShapeDtypeStruct((nbat, rows, bpb * tm), dt)
    out_specs = [hp_spec, t_spec(LANES), t_spec(LANES), hp_spec, t_spec(LANES),
                 col_spec(cw), col_spec(cw), col_spec(gw), col_spec(gw),
                 t_spec(N_HEADS)]
    hp_shape = jax.ShapeDtypeStruct((N_HP, n, LANES), BF16)
    out_shape = [hp_shape, t_shape(D_ATTN), t_shape(D_ATTN), hp_shape, t_shape(D_ATTN, BF16),
                 jax.ShapeDtypeStruct((n, D_CONV), F32), jax.ShapeDtypeStruct((n, D_CONV), BF16),
                 jax.ShapeDtypeStruct((n, D_MODEL), BF16), jax.ShapeDtypeStruct((n, D_MODEL), BF16),
                 t_shape(N_HEADS)]
    return pl.pallas_call(
        functools.partial(_in_proj_kernel, tm=tm, blocks_per_batch=bpb, per_row=per_row,
                          rep=rows_per_batch if per_row else 1,
                          q_scale=ATTN_SCALE if per_row else ATTN_SCALE * LOG2E),
        grid=(nblk, N_CHUNK),
        in_specs=in_specs, out_specs=out_specs, out_shape=out_shape,
        scratch_shapes=scratch,
        compiler_params=_params(2),
        name="in_proj_sample" if per_row else "in_proj_prompt",
    )(*args)


def _lane_cumsum(x):
    n = x.shape[-1]
    lane = lax.broadcasted_iota(jnp.int32, x.shape, x.ndim - 1)
    s = 1
    while s < n:
        x = x + jnp.where(lane >= s, pltpu.roll(x, s, x.ndim - 1), 0.0)
        s *= 2
    return x


def _cumsum_kernel(lft_ref, cumt_ref):
    cumt_ref[0] = _lane_cumsum(lft_ref[0]) * LOG2E


def _cumsum(lft, n_batch, t):
    spec = pl.BlockSpec((1, N_HEADS, t), lambda b: (b, 0, 0))
    return pl.pallas_call(
        _cumsum_kernel,
        grid=(n_batch,),
        in_specs=[spec], out_specs=spec,
        out_shape=jax.ShapeDtypeStruct((n_batch, N_HEADS, t), F32),
        compiler_params=_params(1),
        name="cumsum",
    )(lft)


ACC_ROWS = HEAD_DIM + 16


def _prompt_attn_kernel(qi_ref, kj_ref, q_ref, k_ref, vt_ref, cq_ref, ck_ref, o_ref,
                        acc_scr, m_scr, lg_scr, st_scr, *, tq, tk):
    t = pl.program_id(1)
    qi = qi_ref[t]
    kj = kj_ref[t]
    key0 = kj * tk - qi * tq

    @pl.when(kj == 0)
    def _():
        acc_scr[...] = jnp.zeros_like(acc_scr)
        m_scr[...] = jnp.full_like(m_scr, -jnp.inf)

    lane = lax.broadcasted_iota(jnp.int32, (1, LANES), 1)
    lo_half = lane < HEAD_DIM
    ones_rows = jnp.ones((ACC_ROWS - HEAD_DIM, tk), BF16)
    row8 = lax.broadcasted_iota(jnp.int32, (8, tq), 0)

    def logits(hp, slot, masked):
        q2 = q_ref[hp]
        k2 = k_ref[hp]
        m_pair = m_scr[hp]
        m_out = []
        stats = []
        for e in range(2):
            head = 2 * hp + e
            sel = lo_half if e == 0 else jnp.logical_not(lo_half)
            km = jnp.where(sel, k2, jnp.zeros_like(k2))
            st = lax.dot_general(km, q2, (((1,), (1,)), ((), ())),
                                 preferred_element_type=F32)
            ck_col = jnp.broadcast_to(ck_ref[0, head:head + 1, :], (LANES, tk)).T
            lg = st - pltpu.repeat(ck_col, tq // LANES, axis=1)
            if masked:
                key = lax.broadcasted_iota(jnp.int32, lg.shape, 0) + key0
                qry = lax.broadcasted_iota(jnp.int32, lg.shape, 1)
                lg = jnp.where(key <= qry, lg, -jnp.inf)
            cq = cq_ref[0, head:head + 1, :]
            m_prev = m_pair[e:e + 1, :]
            m_new = jnp.maximum(m_prev, jnp.max(lg, axis=0, keepdims=True) + cq)
            lg_scr[slot, e] = lg
            stats += [cq - m_new, jnp.exp2(m_prev - m_new)]
            m_out.append(m_new)
        st_scr[slot] = jnp.where(row8 == 0, stats[0], jnp.where(
            row8 == 1, stats[1], jnp.where(row8 == 2, stats[2], stats[3])))
        m_scr[hp] = jnp.where(row8 == 0, m_out[0], m_out[1])

    def update(hp, slot):
        vt_pair = vt_ref[0, hp * LANES:(hp + 1) * LANES, :]
        acc_pair = acc_scr[hp]
        stat = st_scr[slot]
        acc_out = []
        for e in range(2):
            p = jnp.exp2(lg_scr[slot, e] + stat[2 * e:2 * e + 1, :]).astype(BF16)
            vaug = jnp.concatenate(
                [vt_pair[e * HEAD_DIM:(e + 1) * HEAD_DIM, :], ones_rows], axis=0)
            acc_out.append(acc_pair[e * ACC_ROWS:(e + 1) * ACC_ROWS, :]
                           * stat[2 * e + 1:2 * e + 2, :]
                           + jnp.dot(vaug, p, preferred_element_type=F32))
        acc_scr[hp] = jnp.concatenate(acc_out, axis=0)

    def all_pairs(masked):
        logits(0, 0, masked)
        for hp in range(N_HP):
            update(hp, hp % 2)
            if hp + 1 < N_HP:
                logits(hp + 1, (hp + 1) % 2, masked)

    @pl.when(key0 + tk <= 0)
    def _():
        all_pairs(False)

    @pl.when(key0 + tk > 0)
    def _():
        all_pairs(True)

    @pl.when(key0 + tk == tq)
    def _():
        def fin(hp, carry):
            acc_pair = acc_scr[hp]
            halves = []
            for e in range(2):
                a = acc_pair[e * ACC_ROWS:(e + 1) * ACC_ROWS, :]
                halves.append(a[0:HEAD_DIM, :] * (1.0 / a[HEAD_DIM:HEAD_DIM + 1, :]))
            o_ref[hp] = jnp.concatenate(halves, axis=0).T.astype(BF16)
            return carry
        lax.fori_loop(0, N_HP, fin, 0)


def _prompt_attn(q_hp, k_hp, vt_b, cumt, n_batch, t, tq, tk):
    nq = t // tq
    nk = t // tk
    pairs = [(a, b) for a in range(nq) for b in range((a + 1) * tq // tk)]
    qi_tab = jnp.asarray([p[0] for p in pairs], jnp.int32)
    kj_tab = jnp.asarray([p[1] for p in pairs], jnp.int32)
    qblk = (N_HP, tq, LANES)
    grid_spec = pltpu.PrefetchScalarGridSpec(
        num_scalar_prefetch=2,
        grid=(n_batch, len(pairs)),
        in_specs=[
            pl.BlockSpec(qblk, lambda b, s, qi, kj: (0, b * nq + qi[s], 0)),
            pl.BlockSpec((N_HP, tk, LANES), lambda b, s, qi, kj: (0, b * nk + kj[s], 0)),
            pl.BlockSpec((1, D_ATTN, tk), lambda b, s, qi, kj: (b, 0, kj[s])),
            pl.BlockSpec((1, N_HEADS, tq), lambda b, s, qi, kj: (b, 0, qi[s])),
            pl.BlockSpec((1, N_HEADS, tk), lambda b, s, qi, kj: (b, 0, kj[s])),
        ],
        out_specs=pl.BlockSpec(qblk, lambda b, s, qi, kj: (0, b * nq + qi[s], 0)),
        scratch_shapes=[pltpu.VMEM((N_HP, 2 * ACC_ROWS, tq), F32),
                        pltpu.VMEM((N_HP, 8, tq), F32),
                        pltpu.VMEM((2, 2, tk, tq), F32),
                        pltpu.VMEM((2, 8, tq), F32)],
    )
    return pl.pallas_call(
        functools.partial(_prompt_attn_kernel, tq=tq, tk=tk),
        grid_spec=grid_spec,
        out_shape=jax.ShapeDtypeStruct((N_HP, n_batch * t, LANES), BF16),
        compiler_params=_params(2),
        name="prompt_attn",
    )(qi_tab, kj_tab, q_hp, k_hp, vt_b, cumt, cumt)


def _sample_attn_kernel(*refs, n_pages, n_new):
    pt_ref = refs[0]
    q_ref, kn_ref, vn_ref, lfn_ref = refs[1:5]
    k_refs = refs[5:5 + n_pages]
    v_refs = refs[5 + n_pages:5 + 2 * n_pages]
    lf_refs = refs[5 + 2 * n_pages:5 + 3 * n_pages]
    o_ref = refs[5 + 3 * n_pages]
    del pt_ref
    nr = n_new * N_HEADS
    nt = ((1,), (1,)), ((), ())
    page_t = lambda ref: ref[0].reshape(D_ATTN, PAGE_SIZE).astype(BF16)

    q = q_ref[0]
    hrow = lax.broadcasted_iota(jnp.int32, (N_HEADS, D_ATTN), 0)
    hlane = lax.broadcasted_iota(jnp.int32, (N_HEADS, D_ATTN), 1) // HEAD_DIM
    hmask = hrow == hlane
    qbd = jnp.concatenate(
        [jnp.where(hmask, jnp.broadcast_to(q[qq:qq + 1, :], (N_HEADS, D_ATTN)), 0.0)
         for qq in range(n_new)], axis=0).astype(BF16)

    s_parts = []
    lf_parts = []
    for c in range(n_pages):
        s_parts.append(jnp.dot(qbd, page_t(k_refs[c]), preferred_element_type=F32))
        lf_parts.append(lf_refs[c][0])
    s_past = jnp.concatenate(s_parts, axis=1)
    ck_past = _lane_cumsum(jnp.concatenate(lf_parts, axis=1))
    n_past = ck_past.shape[1]
    c_total = ck_past[:, n_past - 1:n_past]

    kn = jnp.concatenate(
        [kn_ref[0], jnp.zeros((PAGE_SIZE - 8, D_ATTN), F32)], axis=0).astype(BF16)
    vn = jnp.concatenate(
        [vn_ref[0], jnp.zeros((PAGE_SIZE - 8, D_ATTN), F32)], axis=0).astype(BF16)
    s_new = lax.dot_general(qbd, kn, nt, preferred_element_type=F32)
    ck_new = c_total + _lane_cumsum(lfn_ref[0])

    tile = lambda a: jnp.concatenate([a] * n_new, axis=0)
    ck_past_r = tile(ck_past)
    ck_new_r = tile(ck_new)
    qq_row = lax.broadcasted_iota(jnp.int32, (nr, PAGE_SIZE), 0) // N_HEADS
    lane = lax.broadcasted_iota(jnp.int32, (nr, PAGE_SIZE), 1)
    cq = jnp.sum(jnp.where(lane == qq_row, ck_new_r, 0.0), axis=1, keepdims=True)

    lg_past = s_past + (cq - ck_past_r)
    lg_new = jnp.where(lane <= qq_row, s_new + (cq - ck_new_r), -jnp.inf)
    m = jnp.maximum(jnp.max(lg_past, axis=1, keepdims=True),
                    jnp.max(lg_new, axis=1, keepdims=True))
    p_past = jnp.exp(lg_past - m)
    p_new = jnp.exp(lg_new - m)
    denom = jnp.sum(p_past, axis=1, keepdims=True) + jnp.sum(p_new, axis=1, keepdims=True)
    acc = jnp.dot(p_new.astype(BF16), vn, preferred_element_type=F32)
    pb = p_past.astype(BF16)
    for c in range(n_pages):
        acc = acc + lax.dot_general(pb[:, c * PAGE_SIZE:(c + 1) * PAGE_SIZE], page_t(v_refs[c]),
                                    nt, preferred_element_type=F32)
    acc = acc / denom
    orow = lax.broadcasted_iota(jnp.int32, (8, D_ATTN), 0)
    out = jnp.zeros((8, D_ATTN), F32)
    for qq in range(n_new):
        blk = jnp.where(hmask, acc[qq * N_HEADS:(qq + 1) * N_HEADS, :], 0.0)
        out = jnp.where(orow == qq, jnp.sum(blk, axis=0, keepdims=True), out)
    o_ref[0] = out


def _sample_attn(page_table, q8, kn8, vn8, lfn_t, cache_kt, cache_vt, cache_lft, n_new):
    n_batch, n_pages = page_table.shape
    row_spec = lambda w: pl.BlockSpec((1, 8, w), lambda b, pt: (b, 0, 0))
    kv_page = lambda c: pl.BlockSpec((1, N_HEADS, HEAD_DIM, PAGE_SIZE),
                                     lambda b, pt: (pt[b * n_pages + c], 0, 0, 0))
    lf_page = lambda c: pl.BlockSpec((1, N_HEADS, PAGE_SIZE),
                                     lambda b, pt: (pt[b * n_pages + c], 0, 0))
    in_specs = [row_spec(D_ATTN), row_spec(D_ATTN), row_spec(D_ATTN),
                pl.BlockSpec((1, N_HEADS, LANES), lambda b, pt: (b, 0, 0))]
    in_specs += [kv_page(c) for c in range(n_pages)]
    in_specs += [kv_page(c) for c in range(n_pages)]
    in_specs += [lf_page(c) for c in range(n_pages)]
    grid_spec = pltpu.PrefetchScalarGridSpec(
        num_scalar_prefetch=1,
        grid=(n_batch,),
        in_specs=in_specs,
        out_specs=row_spec(D_ATTN),
    )
    return pl.pallas_call(
        functools.partial(_sample_attn_kernel, n_pages=n_pages, n_new=n_new),
        grid_spec=grid_spec,
        out_shape=jax.ShapeDtypeStruct((n_batch, 8, D_ATTN), F32),
        compiler_params=_params(1),
        name="sample_attn",
    )(page_table.reshape(-1), q8, kn8, vn8, lfn_t,
      *([cache_kt] * n_pages), *([cache_vt] * n_pages), *([cache_lft] * n_pages))


def _merge_kernel(o_ref, yb_ref, sga_ref, sgb_ref, x_ref, gt_ref, sh_ref, sc_ref,
                  gp1_ref, gp2_ref, woa_ref, wob_ref, wo_ref, x1_ref, h2_ref, *, rep):
    tm = x_ref.shape[0]
    gate, shift, scale = _mod(gt_ref, rep), _mod(sh_ref, rep), _mod(sc_ref, rep)
    n_split = 2 if tm % 512 == 0 else 1
    for r in range(n_split):
        rows = slice(r * tm // n_split, (r + 1) * tm // n_split)
        sub = lambda v: v if v.shape[0] == 1 else v[rows, :]
        o = jnp.concatenate([o_ref[hp, rows, :] for hp in range(N_HP)], axis=1)
        a = jnp.dot(o, woa_ref[...], preferred_element_type=F32)
        b = jnp.dot(yb_ref[rows, :], wob_ref[...], preferred_element_type=F32)
        mm = (sga_ref[rows, :].astype(F32) * a + sgb_ref[rows, :].astype(F32) * b).astype(BF16)
        m = jnp.dot(mm, wo_ref[...], preferred_element_type=F32)
        x1 = x_ref[rows, :] + sub(gate) * _rms(m, gp1_ref[...])
        x1_ref[rows, :] = x1
        h2 = _rms(x1, gp2_ref[...]) * (1.0 + sub(scale)) + sub(shift)
        h2_ref[rows, :] = h2.astype(BF16)


def _merge(o_hp, yb, sga, sgb, x, mods, g_post1, g_pre2, w_oa, w_ob, w_o, *, tm,
           rows_per_batch, per_row):
    n = x.shape[0]
    if per_row:
        mod_spec = lambda c: pl.BlockSpec((tm // rows_per_batch, D_MODEL), lambda i: (i, c))
    else:
        bpb = rows_per_batch // tm
        mod_spec = lambda c: pl.BlockSpec((1, 1, D_MODEL), lambda i: (i // bpb, 0, c))
    row_spec = lambda w: pl.BlockSpec((tm, w), lambda i: (i, 0))
    const = lambda shape: pl.BlockSpec(shape, lambda i: (0,) * len(shape),
                                       pipeline_mode=pl.Buffered(1))
    return pl.pallas_call(
        functools.partial(_merge_kernel, rep=rows_per_batch if per_row else 1),
        grid=(n // tm,),
        in_specs=[pl.BlockSpec((N_HP, tm, LANES), lambda i: (0, i, 0)),
                  row_spec(D_CONV), row_spec(D_MODEL), row_spec(D_MODEL), row_spec(D_MODEL),
                  mod_spec(2), mod_spec(3), mod_spec(4),
                  const((1, D_MODEL)), const((1, D_MODEL)),
                  const((D_ATTN, D_MODEL)), const((D_CONV, D_MODEL)),
                  const((D_MODEL, D_MODEL))],
        out_specs=[row_spec(D_MODEL), row_spec(D_MODEL)],
        out_shape=[jax.ShapeDtypeStruct((n, D_MODEL), F32),
                   jax.ShapeDtypeStruct((n, D_MODEL), BF16)],
        compiler_params=_params(1),
        name="merge_sample" if per_row else "merge_prompt",
    )(o_hp, yb, sga, sgb, x, mods, mods, mods, g_post1, g_pre2, w_oa, w_ob, w_o)


def _ffn_kernel(h_ref, wg_ref, wu_ref, wout_ref, x1_ref, gt_ref, gp_ref, y_ref, acc_scr,
                *, rep):
    j = pl.program_id(1)

    @pl.when(j == 0)
    def _():
        acc_scr[...] = jnp.zeros_like(acc_scr)

    h = h_ref[...]
    g = jnp.dot(h, wg_ref[...], preferred_element_type=F32)
    u = jnp.dot(h, wu_ref[...], preferred_element_type=F32)
    act = (g * jax.nn.sigmoid(g) * u).astype(BF16)
    acc_scr[...] += jnp.dot(act, wout_ref[...], preferred_element_type=F32)

    @pl.when(j == pl.num_programs(1) - 1)
    def _():
        y_ref[...] = x1_ref[...] + _mod(gt_ref, rep) * _rms(acc_scr[...], gp_ref[...])


def _ffn(h2, x1, mods, g_post2, w_ffn_in, w_ffn_out, *, tm, tf, rows_per_batch, per_row):
    n = x1.shape[0]
    nf = D_FF // tf
    if per_row:
        mod_spec = pl.BlockSpec((tm // rows_per_batch, D_MODEL), lambda i, j: (i, 5))
    else:
        bpb = rows_per_batch // tm
        mod_spec = pl.BlockSpec((1, 1, D_MODEL), lambda i, j: (i // bpb, 0, 5))
    row_spec = pl.BlockSpec((tm, D_MODEL), lambda i, j: (i, 0))
    return pl.pallas_call(
        functools.partial(_ffn_kernel, rep=rows_per_batch if per_row else 1),
        grid=(n // tm, nf),
        in_specs=[row_spec,
                  pl.BlockSpec((D_MODEL, tf), lambda i, j: (0, j)),
                  pl.BlockSpec((D_MODEL, tf), lambda i, j: (0, j + nf)),
                  pl.BlockSpec((tf, D_MODEL), lambda i, j: (j, 0)),
                  row_spec, mod_spec,
                  pl.BlockSpec((1, D_MODEL), lambda i, j: (0, 0))],
        out_specs=row_spec,
        out_shape=jax.ShapeDtypeStruct((n, D_MODEL), F32),
        scratch_shapes=[pltpu.VMEM((tm, D_MODEL), F32)],
        compiler_params=_params(2),
        name="ffn_sample" if per_row else "ffn_prompt",
    )(h2, w_ffn_in, w_ffn_in, w_ffn_out, x1, mods, g_post2)


def _w_chunk_kernel(tab_ref, *refs):
    del tab_ref
    f_ref, o_ref, of_ref = refs[-3:]
    for pos, w_ref in enumerate(refs[:-3]):
        o_ref[:, pos * LANES:(pos + 1) * LANES] = w_ref[...].T.astype(BF16)

    @pl.when(pl.program_id(0) == 0)
    def _():
        ft = f_ref[...].T
        lane = lax.broadcasted_iota(jnp.int32, ft.shape, 1)
        of_ref[...] = jnp.where(lane < N_HEADS, ft, 0.0).astype(BF16)


def _chunked_in_weights(w_in):
    d = w_in.shape[0]
    offs = np.cumsum([0, D_ATTN, D_ATTN, D_ATTN, N_HEADS, D_CONV, D_CONV, D_CONV,
                      D_MODEL, D_MODEL])
    q, k, v, f, u, bg, cg, ga, gb = [int(o) for o in offs[:9]]
    cw = D_ATTN // N_CHUNK
    gw = D_MODEL // N_CHUNK
    starts = []
    for j in range(N_CHUNK):
        starts += [u + cw * j, cg + cw * j, bg + cw * j]
        starts += [ga + gw * j + s for s in range(0, gw, LANES)]
        starts += [gb + gw * j + s for s in range(0, gw, LANES)]
        starts += [q + cw * j, k + cw * j, v + cw * j]
    n_slab = CHUNK_W // LANES
    slab = lambda pos: pl.BlockSpec((pl.Element(LANES), pl.Element(d)),
                                    lambda j, tab: (pl.multiple_of(tab[j * n_slab + pos], 8), 0))
    f_slab = pl.BlockSpec((pl.Element(LANES), pl.Element(d)), lambda j, tab: (f, 0))
    return pl.pallas_call(
        _w_chunk_kernel,
        grid_spec=pltpu.PrefetchScalarGridSpec(
            num_scalar_prefetch=1, grid=(N_CHUNK,),
            in_specs=[slab(pos) for pos in range(n_slab)] + [f_slab],
            out_specs=[pl.BlockSpec((d, CHUNK_W), lambda j, tab: (0, j)),
                       pl.BlockSpec((d, LANES), lambda j, tab: (0, 0))]),
        out_shape=[jax.ShapeDtypeStruct((d, N_CHUNK * CHUNK_W), BF16),
                   jax.ShapeDtypeStruct((d, LANES), BF16)],
        compiler_params=_params(1),
        name="w_chunk",
    )(jnp.asarray(starts, jnp.int32), *([w_in.T] * (n_slab + 1)))


def kernel(x_prompt, x_sample, cache_k, cache_v, cache_logf, state_conv, page_table,
           c_prompt, c_sample, w_ada, b_ada, g_pre1, w_in, b_f, w_conv, w_oa, w_ob, w_o,
           g_post1, g_pre2, w_ffn_in, w_ffn_out, g_post2):
    depth = w_in.shape[0]
    assert depth == 1
    nb, t, d = x_prompt.shape
    db, ts, _ = x_sample.shape
    l = 0

    w_main, w_f = _chunked_in_weights(w_in[l])
    b_f_pad = jnp.pad(b_f[l], (0, LANES - N_HEADS)).reshape(1, LANES)
    w_oa_b = w_oa[l].astype(BF16)
    w_ob_b = w_ob[l].astype(BF16)
    w_o_b = w_o[l].astype(BF16)
    w_fin_b = w_ffn_in[l].astype(BF16)
    w_fout_b = w_ffn_out[l].astype(BF16)
    row2 = lambda a: a.reshape(1, -1)

    mods = _adaln(jnp.concatenate([c_prompt, c_sample], axis=0), w_ada[l], b_ada[l])
    mods_p = mods[:nb].reshape(nb, 1, 6 * d)
    mods_s = mods[nb:]

    xp = x_prompt.reshape(nb * t, d)
    (q_hp, kt, vt, k_hp, vt_b, zc, yb, sga, sgb, lft) = _in_proj(
        xp, row2(g_pre1[l]), mods_p, mods_p, w_main, w_f, b_f_pad, w_conv[l], None,
        tm=1024, rows_per_batch=t)
    cumt = _cumsum(lft, nb, t)
    o_hp = _prompt_attn(q_hp, k_hp, vt_b, cumt, nb, t, 512, 512)
    x1, h2 = _merge(o_hp, yb, sga, sgb, xp, mods_p, row2(g_post1[l]), row2(g_pre2[l]),
                    w_oa_b, w_ob_b, w_o_b, tm=512, rows_per_batch=t, per_row=False)
    yp = _ffn(h2, x1, mods_p, row2(g_post2[l]), w_fin_b, w_fout_b, tm=512, tf=512,
              rows_per_batch=t, per_row=False)

    ns = db * ts
    xs = x_sample.reshape(ns, d)
    st = state_conv[l].astype(F32)
    state = (jnp.repeat(st[:, 0, :], ts, axis=0), jnp.repeat(st[:, 1, :], ts, axis=0))
    (qs_hp, kst, vst, _, _, zcs, ybs, sgas, sgbs, lfst) = _in_proj(
        xs, row2(g_pre1[l]), mods_s, mods_s, w_main, w_f, b_f_pad, w_conv[l], state,
        tm=ns, rows_per_batch=ts)
    ks_f = kst[0].T
    vs_f = vst[0].T
    lfs = lfst[0].reshape(N_HEADS, db, ts)
    pad8 = lambda a: jnp.pad(a.reshape(db, ts, -1), ((0, 0), (0, 8 - ts), (0, 0)))
    qs = qs_hp.transpose(1, 0, 2).reshape(ns, D_ATTN).astype(F32)
    lfn_t = jnp.pad(lfs.transpose(1, 0, 2), ((0, 0), (0, 0), (0, LANES - ts)))
    os8 = _sample_attn(page_table, pad8(qs), pad8(ks_f), pad8(vs_f), lfn_t,
                       cache_k[l].transpose(0, 2, 3, 1), cache_v[l].transpose(0, 2, 3, 1),
                       cache_logf[l].transpose(0, 2, 1), ts)
    os_hp = (os8[:, :ts, :].reshape(ns, N_HP, LANES).transpose(1, 0, 2).astype(BF16))
    x1s, h2s = _merge(os_hp, ybs, sgas, sgbs, xs, mods_s, row2(g_post1[l]), row2(g_pre2[l]),
                      w_oa_b, w_ob_b, w_o_b, tm=ns, rows_per_batch=ts, per_row=True)
    ys = _ffn(h2s, x1s, mods_s, row2(g_post2[l]), w_fin_b, w_fout_b, tm=ns, tf=512,
              rows_per_batch=ts, per_row=True)

    heads = lambda a, b_, t_: a.reshape(1, b_, t_, N_HEADS, HEAD_DIM)
    heads_t = lambda a: a.reshape(nb, N_HEADS, HEAD_DIM, t).transpose(0, 3, 1, 2)[None]
    return (yp.reshape(nb, t, d), ys.reshape(db, ts, d),
            heads_t(kt), heads_t(vt), lft.transpose(0, 2, 1)[None],
            zc.reshape(nb, t, D_CONV)[:, t - (CONV_W - 1):, :][None],
            heads(ks_f, db, ts), heads(vs_f, db, ts), lfs.transpose(1, 2, 0)[None],
            zcs.reshape(db, ts, D_CONV)[:, ts - (CONV_W - 1):, :][None])
```

```python
import functools

import jax
import jax.numpy as jnp
import numpy as np
from jax import lax
from jax.experimental import pallas as pl
from jax.experimental.pallas import tpu as pltpu

F32 = jnp.float32
BF16 = jnp.bfloat16

D_MODEL = 2048
N_HEADS = 16
HEAD_DIM = 64
D_ATTN = N_HEADS * HEAD_DIM
D_CONV = 1024
CONV_W = 3
D_FF = 5632
RMS_EPS = 1e-6
ATTN_SCALE = HEAD_DIM ** -0.5
LOG2E = 1.4426950408889634
PAGE_SIZE = 128

LANES = 128
N_HP = D_ATTN // LANES
N_CHUNK = 8
CHUNK_W = 6 * (D_ATTN // N_CHUNK) + 2 * (D_MODEL // N_CHUNK)
VMEM_LIMIT = 56 * 1024 * 1024


def _params(n_axes, vmem=VMEM_LIMIT, flags=None):
    return pltpu.CompilerParams(
        dimension_semantics=("arbitrary",) * n_axes, vmem_limit_bytes=vmem, flags=flags)


def _rms(x, g):
    return x * lax.rsqrt(jnp.mean(x * x, axis=-1, keepdims=True) + RMS_EPS) * g


def _repeat_rows(m, rep):
    n = m.shape[0]
    assert rep & (rep - 1) == 0
    src = lax.broadcasted_iota(jnp.int32, (n * rep, n), 0) >> (rep.bit_length() - 1)
    col = lax.broadcasted_iota(jnp.int32, (n * rep, n), 1)
    onehot = jnp.where(src == col, 1.0, 0.0).astype(BF16)
    out = None
    for _ in range(3):
        piece = m.astype(BF16)
        part = jnp.dot(onehot, piece, preferred_element_type=F32)
        out = part if out is None else out + part
        m = m - piece.astype(F32)
    return out


def _mod(ref, rep=1):
    if len(ref.shape) == 3:
        return ref[0]
    return ref[...] if rep == 1 else _repeat_rows(ref[...], rep)


def _log_sigmoid(z):
    return jnp.minimum(z, 0.0) - jnp.log1p(jnp.exp(-jnp.abs(z)))


def _adaln_kernel(c_ref, w_ref, b_ref, o_ref):
    @pl.when(pl.program_id(0) == 0)
    def _():
        o_ref[...] = jnp.broadcast_to(b_ref[...], o_ref.shape)

    c = c_ref[...]
    s = (c * jax.nn.sigmoid(c)).astype(BF16)
    o_ref[...] += jnp.dot(s, w_ref[...].astype(BF16), preferred_element_type=F32)


def _adaln(c, w_ada, b_ada):
    m, d = c.shape
    n = w_ada.shape[1]
    tk = LANES
    return pl.pallas_call(
        _adaln_kernel,
        grid=(d // tk,),
        in_specs=[pl.BlockSpec((m, tk), lambda k: (0, k)),
                  pl.BlockSpec((tk, n), lambda k: (k, 0)),
                  pl.BlockSpec((1, n), lambda k: (0, 0))],
        out_specs=pl.BlockSpec((m, n), lambda k: (0, 0)),
        out_shape=jax.ShapeDtypeStruct((m, n), F32),
        compiler_params=_params(1),
        name="adaln",
    )(c, w_ada, b_ada.reshape(1, n))


def _in_proj_kernel(*refs, tm, blocks_per_batch, per_row, rep, q_scale):
    if per_row:
        (x_ref, g_ref, sh_ref, sc_ref, w_ref, wf_ref, bf_ref, wc_ref, s0_ref, s1_ref,
         q_ref, kt_ref, vt_ref, kb_ref, vtb_ref, zc_ref, yb_ref, sga_ref, sgb_ref,
         lft_ref, h_scr) = refs
    else:
        (x_ref, g_ref, sh_ref, sc_ref, w_ref, wf_ref, bf_ref, wc_ref,
         q_ref, kt_ref, vt_ref, kb_ref, vtb_ref, zc_ref, yb_ref, sga_ref, sgb_ref,
         lft_ref, h_scr, halo_scr) = refs
    i = pl.program_id(0)
    j = pl.program_id(1)

    @pl.when(j == 0)
    def _():
        h = _rms(x_ref[...], g_ref[...]) * (1.0 + _mod(sc_ref, rep)) + _mod(sh_ref, rep)
        hb = h.astype(BF16)
        h_scr[...] = hb
        f = jnp.dot(hb, wf_ref[...], preferred_element_type=F32)
        lf = _log_sigmoid(f + bf_ref[...])
        lft_ref[0] = lf.T[0:N_HEADS, :]

    res = jnp.dot(h_scr[...], w_ref[...], preferred_element_type=F32)
    cw = D_ATTN // N_CHUNK
    gw = D_MODEL // N_CHUNK
    u = res[:, 0:cw]
    cg = res[:, cw:2 * cw]
    bg = res[:, 2 * cw:3 * cw]
    ga = res[:, 3 * cw:3 * cw + gw]
    gb = res[:, 3 * cw + gw:3 * cw + 2 * gw]
    o = 3 * cw + 2 * gw
    q = res[:, o:o + cw]
    k = res[:, o + cw:o + 2 * cw]
    v = res[:, o + 2 * cw:o + 3 * cw]

    q_ref[0] = (q * q_scale).astype(BF16)
    kt_ref[0] = k.T
    vt = v.T
    vt_ref[0] = vt
    vtb_ref[0] = vt.astype(BF16)
    kb_ref[0] = k.astype(BF16)
    sga_ref[...] = jax.nn.sigmoid(ga).astype(BF16)
    sgb_ref[...] = jax.nn.sigmoid(gb).astype(BF16)

    zc = cg * u
    zc_ref[...] = zc
    row = lax.broadcasted_iota(jnp.int32, zc.shape, 0)
    r1 = pltpu.roll(zc, 1, 0)
    r2 = pltpu.roll(zc, 2, 0)
    if per_row:
        t = row & 3
        s0 = s0_ref[...]
        s1 = s1_ref[...]
        p1 = jnp.where(t >= 1, r1, s1)
        p2 = jnp.where(t >= 2, r2, jnp.where(t == 1, s1, s0))
    else:
        first = (i % blocks_per_batch) == 0
        halo = halo_scr[j]
        h6 = jnp.where(first, 0.0, halo[6:7, :])
        h7 = jnp.where(first, 0.0, halo[7:8, :])
        p1 = jnp.where(row == 0, h7, r1)
        p2 = jnp.where(row == 0, h6, jnp.where(row == 1, h7, r2))
        halo_scr[j] = zc[tm - 8:tm, :]
    wc = wc_ref[...]
    y = p2 * wc[0:1, :]
    y = y + p1 * wc[1:2, :]
    y = y + zc * wc[2:3, :]
    yb_ref[...] = (bg * y).astype(BF16)


def _in_proj(x, g_pre, shift, scale, w_main, w_f, b_f, w_conv, state, *, tm, rows_per_batch):
    n = x.shape[0]
    per_row = state is not None
    nblk = n // tm
    cw = D_ATTN // N_CHUNK
    gw = D_MODEL // N_CHUNK
    if per_row:
        bpb = 1
        mod_spec = lambda c: pl.BlockSpec((tm // rows_per_batch, D_MODEL), lambda i, j: (i, c))
    else:
        bpb = rows_per_batch // tm
        mod_spec = lambda c: pl.BlockSpec((1, 1, D_MODEL), lambda i, j: (i // bpb, 0, c))
    in_specs = [
        pl.BlockSpec((tm, D_MODEL), lambda i, j: (i, 0)),
        pl.BlockSpec((1, D_MODEL), lambda i, j: (0, 0)),
        mod_spec(0), mod_spec(1),
        pl.BlockSpec((D_MODEL, CHUNK_W), lambda i, j: (0, j)),
        pl.BlockSpec((D_MODEL, LANES), lambda i, j: (0, 0)),
        pl.BlockSpec((1, LANES), lambda i, j: (0, 0)),
        pl.BlockSpec((CONV_W, cw), lambda i, j: (0, j)),
    ]
    args = [x, g_pre, shift, scale, w_main, w_f, b_f, w_conv]
    scratch = [pltpu.VMEM((tm, D_MODEL), BF16)]
    if per_row:
        in_specs += [pl.BlockSpec((tm, cw), lambda i, j: (i, j))] * 2
        args += [state[0], state[1]]
    else:
        scratch.append(pltpu.VMEM((N_CHUNK, 8, cw), F32))
    hp_spec = pl.BlockSpec((1, tm, LANES), lambda i, j: (j, i, 0))
    col_spec = lambda w: pl.BlockSpec((tm, w), lambda i, j: (i, j))
    nbat = nblk // bpb
    t_spec = lambda rows: pl.BlockSpec(
        (1, rows, tm), lambda i, j: (i // bpb, j if rows == LANES else 0, i % bpb))
    t_shape = lambda rows, dt=F32: jax.ShapeDtypeStruct((nbat, rows, bpb * tm), dt)
    out_specs = [hp_spec, t_spec(LANES), t_spec(LANES), hp_spec, t_spec(LANES),
                 col_spec(cw), col_spec(cw), col_spec(gw), col_spec(gw),
                 t_spec(N_HEADS)]
    hp_shape = jax.ShapeDtypeStruct((N_HP, n, LANES), BF16)
    out_shape = [hp_shape, t_shape(D_ATTN), t_shape(D_ATTN), hp_shape, t_shape(D_ATTN, BF16),
                 jax.ShapeDtypeStruct((n, D_CONV), F32), jax.ShapeDtypeStruct((n, D_CONV), BF16),
                 jax.ShapeDtypeStruct((n, D_MODEL), BF16), jax.ShapeDtypeStruct((n, D_MODEL), BF16),
                 t_shape(N_HEADS)]
    return pl.pallas_call(
        functools.partial(_in_proj_kernel, tm=tm, blocks_per_batch=bpb, per_row=per_row,
                          rep=rows_per_batch if per_row else 1,
                          q_scale=ATTN_SCALE if per_row else ATTN_SCALE * LOG2E),
        grid=(nblk, N_CHUNK),
        in_specs=in_specs, out_specs=out_specs, out_shape=out_shape,
        scratch_shapes=scratch,
        compiler_params=_params(2),
        name="in_proj_sample" if per_row else "in_proj_prompt",
    )(*args)


def _lane_cumsum(x):
    n = x.shape[-1]
    lane = lax.broadcasted_iota(jnp.int32, x.shape, x.ndim - 1)
    s = 1
    while s < n:
        x = x + jnp.where(lane >= s, pltpu.roll(x, s, x.ndim - 1), 0.0)
        s *= 2
    return x


def _cumsum_kernel(lft_ref, cumt_ref):
    cumt_ref[0] = _lane_cumsum(lft_ref[0]) * LOG2E


def _cumsum(lft, n_batch, t):
    spec = pl.BlockSpec((1, N_HEADS, t), lambda b: (b, 0, 0))
    return pl.pallas_call(
        _cumsum_kernel,
        grid=(n_batch,),
        in_specs=[spec], out_specs=spec,
        out_shape=jax.ShapeDtypeStruct((n_batch, N_HEADS, t), F32),
        compiler_params=_params(1),
        name="cumsum",
    )(lft)


ACC_ROWS = HEAD_DIM + 16


def _prompt_attn_kernel(qi_ref, kj_ref, q_ref, k_ref, vt_ref, cq_ref, ck_ref, o_ref,
                        acc_scr, m_scr, lg_scr, st_scr, mask_scr, *, tq, tk):
    t = pl.program_id(1)
    qi = qi_ref[t]
    kj = kj_ref[t]
    key0 = kj * tk - qi * tq

    @pl.when(kj == 0)
    def _():
        acc_scr[...] = jnp.zeros_like(acc_scr)
        m_scr[...] = jnp.full_like(m_scr, -jnp.inf)

    lane = lax.broadcasted_iota(jnp.int32, (1, LANES), 1)
    lo_half = lane < HEAD_DIM
    ones_rows = jnp.ones((ACC_ROWS - HEAD_DIM, tk), BF16)
    row8 = lax.broadcasted_iota(jnp.int32, (8, tq), 0)

    def logits(hp, slot, masked):
        q2 = q_ref[hp]
        k2 = k_ref[hp]
        m_pair = m_scr[hp]
        m_out = []
        stats = []
        for e in range(2):
            head = 2 * hp + e
            sel = lo_half if e == 0 else jnp.logical_not(lo_half)
            km = jnp.where(sel, k2, jnp.zeros_like(k2))
            st = lax.dot_general(km, q2, (((1,), (1,)), ((), ())),
                                 preferred_element_type=F32)
            ck_col = jnp.broadcast_to(ck_ref[0, head:head + 1, :], (LANES, tk)).T
            lg = st - pltpu.repeat(ck_col, tq // LANES, axis=1)
            if masked:
                lg = lg + mask_scr[...]
            cq = cq_ref[0, head:head + 1, :]
            m_prev = m_pair[e:e + 1, :]
            m_new = jnp.maximum(m_prev, jnp.max(lg, axis=0, keepdims=True) + cq)
            lg_scr[slot, e] = lg
            stats += [cq - m_new, jnp.exp2(m_prev - m_new)]
            m_out.append(m_new)
        st_scr[slot] = jnp.where(row8 == 0, stats[0], jnp.where(
            row8 == 1, stats[1], jnp.where(row8 == 2, stats[2], stats[3])))
        m_scr[hp] = jnp.where(row8 == 0, m_out[0], m_out[1])

    def update(hp, slot):
        vt_pair = vt_ref[0, hp * LANES:(hp + 1) * LANES, :]
        acc_pair = acc_scr[hp]
        stat = st_scr[slot]
        acc_out = []
        for e in range(2):
            p = jnp.exp2(lg_scr[slot, e] + stat[2 * e:2 * e + 1, :]).astype(BF16)
            vaug = jnp.concatenate(
                [vt_pair[e * HEAD_DIM:(e + 1) * HEAD_DIM, :], ones_rows], axis=0)
            acc_out.append(acc_pair[e * ACC_ROWS:(e + 1) * ACC_ROWS, :]
                           * stat[2 * e + 1:2 * e + 2, :]
                           + jnp.dot(vaug, p, preferred_element_type=F32))
        acc_scr[hp] = jnp.concatenate(acc_out, axis=0)

    def all_pairs(masked):
        logits(0, 0, masked)
        for hp in range(N_HP):
            if hp + 1 < N_HP:
                logits(hp + 1, (hp + 1) % 2, masked)
            update(hp, hp % 2)

    @pl.when(key0 + tk <= 0)
    def _():
        all_pairs(False)

    @pl.when(key0 + tk > 0)
    def _():
        key = lax.broadcasted_iota(jnp.int32, (tk, tq), 0) + key0
        qry = lax.broadcasted_iota(jnp.int32, (tk, tq), 1)
        mask_scr[...] = jnp.where(key <= qry, 0.0, -jnp.inf)
        all_pairs(True)

    @pl.when(key0 + tk == tq)
    def _():
        def fin(hp, carry):
            acc_pair = acc_scr[hp]
            halves = []
            for e in range(2):
                a = acc_pair[e * ACC_ROWS:(e + 1) * ACC_ROWS, :]
                halves.append(a[0:HEAD_DIM, :] * (1.0 / a[HEAD_DIM:HEAD_DIM + 1, :]))
            o_ref[hp] = jnp.concatenate(halves, axis=0).T.astype(BF16)
            return carry
        lax.fori_loop(0, N_HP, fin, 0)


def _prompt_attn(q_hp, k_hp, vt_b, cumt, n_batch, t, tq, tk):
    nq = t // tq
    nk = t // tk
    pairs = [(a, b) for a in range(nq) for b in range((a + 1) * tq // tk)]
    qi_tab = jnp.asarray([p[0] for p in pairs], jnp.int32)
    kj_tab = jnp.asarray([p[1] for p in pairs], jnp.int32)
    qblk = (N_HP, tq, LANES)
    grid_spec = pltpu.PrefetchScalarGridSpec(
        num_scalar_prefetch=2,
        grid=(n_batch, len(pairs)),
        in_specs=[
            pl.BlockSpec(qblk, lambda b, s, qi, kj: (0, b * nq + qi[s], 0)),
            pl.BlockSpec((N_HP, tk, LANES), lambda b, s, qi, kj: (0, b * nk + kj[s], 0)),
            pl.BlockSpec((1, D_ATTN, tk), lambda b, s, qi, kj: (b, 0, kj[s])),
            pl.BlockSpec((1, N_HEADS, tq), lambda b, s, qi, kj: (b, 0, qi[s])),
            pl.BlockSpec((1, N_HEADS, tk), lambda b, s, qi, kj: (b, 0, kj[s])),
        ],
        out_specs=pl.BlockSpec(qblk, lambda b, s, qi, kj: (0, b * nq + qi[s], 0)),
        scratch_shapes=[pltpu.VMEM((N_HP, 2 * ACC_ROWS, tq), F32),
                        pltpu.VMEM((N_HP, 8, tq), F32),
                        pltpu.VMEM((2, 2, tk, tq), F32),
                        pltpu.VMEM((2, 8, tq), F32),
                        pltpu.VMEM((tk, tq), F32)],
    )
    return pl.pallas_call(
        functools.partial(_prompt_attn_kernel, tq=tq, tk=tk),
        grid_spec=grid_spec,
        out_shape=jax.ShapeDtypeStruct((N_HP, n_batch * t, LANES), BF16),
        compiler_params=_params(2),
        name="prompt_attn",
    )(qi_tab, kj_tab, q_hp, k_hp, vt_b, cumt, cumt)


def _sample_attn_kernel(*refs, n_pages, n_new):
    pt_ref = refs[0]
    q_ref, kn_ref, vn_ref, lfn_ref = refs[1:5]
    k_refs = refs[5:5 + n_pages]
    v_refs = refs[5 + n_pages:5 + 2 * n_pages]
    lf_refs = refs[5 + 2 * n_pages:5 + 3 * n_pages]
    o_ref = refs[5 + 3 * n_pages]
    del pt_ref
    nr = n_new * N_HEADS
    nt = ((1,), (1,)), ((), ())
    page_t = lambda ref: ref[0].reshape(D_ATTN, PAGE_SIZE).astype(BF16)

    q = q_ref[0]
    hrow = lax.broadcasted_iota(jnp.int32, (N_HEADS, D_ATTN), 0)
    hlane = lax.broadcasted_iota(jnp.int32, (N_HEADS, D_ATTN), 1) // HEAD_DIM
    hmask = hrow == hlane
    qbd = jnp.concatenate(
        [jnp.where(hmask, jnp.broadcast_to(q[qq:qq + 1, :], (N_HEADS, D_ATTN)), 0.0)
         for qq in range(n_new)], axis=0).astype(BF16)

    s_parts = []
    lf_parts = []
    for c in range(n_pages):
        s_parts.append(jnp.dot(qbd, page_t(k_refs[c]), preferred_element_type=F32))
        lf_parts.append(lf_refs[c][0])
    s_past = jnp.concatenate(s_parts, axis=1)
    ck_past = _lane_cumsum(jnp.concatenate(lf_parts, axis=1))
    n_past = ck_past.shape[1]
    c_total = ck_past[:, n_past - 1:n_past]

    kn = jnp.concatenate(
        [kn_ref[0], jnp.zeros((PAGE_SIZE - 8, D_ATTN), F32)], axis=0).astype(BF16)
    vn = jnp.concatenate(
        [vn_ref[0], jnp.zeros((PAGE_SIZE - 8, D_ATTN), F32)], axis=0).astype(BF16)
    s_new = lax.dot_general(qbd, kn, nt, preferred_element_type=F32)
    ck_new = c_total + _lane_cumsum(lfn_ref[0])

    tile = lambda a: jnp.concatenate([a] * n_new, axis=0)
    ck_past_r = tile(ck_past)
    ck_new_r = tile(ck_new)
    qq_row = lax.broadcasted_iota(jnp.int32, (nr, PAGE_SIZE), 0) // N_HEADS
    lane = lax.broadcasted_iota(jnp.int32, (nr, PAGE_SIZE), 1)
    cq = jnp.sum(jnp.where(lane == qq_row, ck_new_r, 0.0), axis=1, keepdims=True)

    lg_past = s_past + (cq - ck_past_r)
    lg_new = jnp.where(lane <= qq_row, s_new + (cq - ck_new_r), -jnp.inf)
    m = jnp.maximum(jnp.max(lg_past, axis=1, keepdims=True),
                    jnp.max(lg_new, axis=1, keepdims=True))
    p_past = jnp.exp(lg_past - m)
    p_new = jnp.exp(lg_new - m)
    denom = jnp.sum(p_past, axis=1, keepdims=True) + jnp.sum(p_new, axis=1, keepdims=True)
    acc = jnp.dot(p_new.astype(BF16), vn, preferred_element_type=F32)
    pb = p_past.astype(BF16)
    for c in range(n_pages):
        acc = acc + lax.dot_general(pb[:, c * PAGE_SIZE:(c + 1) * PAGE_SIZE], page_t(v_refs[c]),
                                    nt, preferred_element_type=F32)
    acc = acc / denom
    orow = lax.broadcasted_iota(jnp.int32, (8, D_ATTN), 0)
    out = jnp.zeros((8, D_ATTN), F32)
    for qq in range(n_new):
        blk = jnp.where(hmask, acc[qq * N_HEADS:(qq + 1) * N_HEADS, :], 0.0)
        out = jnp.where(orow == qq, jnp.sum(blk, axis=0, keepdims=True), out)
    o_ref[0] = out


def _sample_attn(page_table, q8, kn8, vn8, lfn_t, cache_kt, cache_vt, cache_lft, n_new):
    n_batch, n_pages = page_table.shape
    row_spec = lambda w: pl.BlockSpec((1, 8, w), lambda b, pt: (b, 0, 0))
    kv_page = lambda c: pl.BlockSpec((1, N_HEADS, HEAD_DIM, PAGE_SIZE),
                                     lambda b, pt: (pt[b * n_pages + c], 0, 0, 0))
    lf_page = lambda c: pl.BlockSpec((1, N_HEADS, PAGE_SIZE),
                                     lambda b, pt: (pt[b * n_pages + c], 0, 0))
    in_specs = [row_spec(D_ATTN), row_spec(D_ATTN), row_spec(D_ATTN),
                pl.BlockSpec((1, N_HEADS, LANES), lambda b, pt: (b, 0, 0))]
    in_specs += [kv_page(c) for c in range(n_pages)]
    in_specs += [kv_page(c) for c in range(n_pages)]
    in_specs += [lf_page(c) for c in range(n_pages)]
    grid_spec = pltpu.PrefetchScalarGridSpec(
        num_scalar_prefetch=1,
        grid=(n_batch,),
        in_specs=in_specs,
        out_specs=row_spec(D_ATTN),
    )
    return pl.pallas_call(
        functools.partial(_sample_attn_kernel, n_pages=n_pages, n_new=n_new),
        grid_spec=grid_spec,
        out_shape=jax.ShapeDtypeStruct((n_batch, 8, D_ATTN), F32),
        compiler_params=_params(1),
        name="sample_attn",
    )(page_table.reshape(-1), q8, kn8, vn8, lfn_t,
      *([cache_kt] * n_pages), *([cache_vt] * n_pages), *([cache_lft] * n_pages))


def _merge_kernel(o_ref, yb_ref, sga_ref, sgb_ref, x_ref, gt_ref, sh_ref, sc_ref,
                  gp1_ref, gp2_ref, woa_ref, wob_ref, wo_ref, x1_ref, h2_ref, *, rep):
    tm = x_ref.shape[0]
    gate, shift, scale = _mod(gt_ref, rep), _mod(sh_ref, rep), _mod(sc_ref, rep)
    n_split = 2 if tm % 512 == 0 else 1
    for r in range(n_split):
        rows = slice(r * tm // n_split, (r + 1) * tm // n_split)
        sub = lambda v: v if v.shape[0] == 1 else v[rows, :]
        o = jnp.concatenate([o_ref[hp, rows, :] for hp in range(N_HP)], axis=1)
        a = jnp.dot(o, woa_ref[...], preferred_element_type=F32)
        b = jnp.dot(yb_ref[rows, :], wob_ref[...], preferred_element_type=F32)
        mm = (sga_ref[rows, :].astype(F32) * a + sgb_ref[rows, :].astype(F32) * b).astype(BF16)
        m = jnp.dot(mm, wo_ref[...], preferred_element_type=F32)
        x1 = x_ref[rows, :] + sub(gate) * _rms(m, gp1_ref[...])
        x1_ref[rows, :] = x1
        h2 = _rms(x1, gp2_ref[...]) * (1.0 + sub(scale)) + sub(shift)
        h2_ref[rows, :] = h2.astype(BF16)


def _merge(o_hp, yb, sga, sgb, x, mods, g_post1, g_pre2, w_oa, w_ob, w_o, *, tm,
           rows_per_batch, per_row):
    n = x.shape[0]
    if per_row:
        mod_spec = lambda c: pl.BlockSpec((tm // rows_per_batch, D_MODEL), lambda i: (i, c))
    else:
        bpb = rows_per_batch // tm
        mod_spec = lambda c: pl.BlockSpec((1, 1, D_MODEL), lambda i: (i // bpb, 0, c))
    row_spec = lambda w: pl.BlockSpec((tm, w), lambda i: (i, 0))
    const = lambda shape: pl.BlockSpec(shape, lambda i: (0,) * len(shape),
                                       pipeline_mode=pl.Buffered(1))
    return pl.pallas_call(
        functools.partial(_merge_kernel, rep=rows_per_batch if per_row else 1),
        grid=(n // tm,),
        in_specs=[pl.BlockSpec((N_HP, tm, LANES), lambda i: (0, i, 0)),
                  row_spec(D_CONV), row_spec(D_MODEL), row_spec(D_MODEL), row_spec(D_MODEL),
                  mod_spec(2), mod_spec(3), mod_spec(4),
                  const((1, D_MODEL)), const((1, D_MODEL)),
                  const((D_ATTN, D_MODEL)), const((D_CONV, D_MODEL)),
                  const((D_MODEL, D_MODEL))],
        out_specs=[row_spec(D_MODEL), row_spec(D_MODEL)],
        out_shape=[jax.ShapeDtypeStruct((n, D_MODEL), F32),
                   jax.ShapeDtypeStruct((n, D_MODEL), BF16)],
        compiler_params=_params(1),
        name="merge_sample" if per_row else "merge_prompt",
    )(o_hp, yb, sga, sgb, x, mods, mods, mods, g_post1, g_pre2, w_oa, w_ob, w_o)


def _ffn_kernel(h_ref, wg_ref, wu_ref, wout_ref, x1_ref, gt_ref, gp_ref, y_ref, acc_scr,
                *, rep):
    j = pl.program_id(1)

    @pl.when(j == 0)
    def _():
        acc_scr[...] = jnp.zeros_like(acc_scr)

    h = h_ref[...]
    g = jnp.dot(h, wg_ref[...], preferred_element_type=F32)
    u = jnp.dot(h, wu_ref[...], preferred_element_type=F32)
    act = (g * jax.nn.sigmoid(g) * u).astype(BF16)
    acc_scr[...] += jnp.dot(act, wout_ref[...], preferred_element_type=F32)

    @pl.when(j == pl.num_programs(1) - 1)
    def _():
        y_ref[...] = x1_ref[...] + _mod(gt_ref, rep) * _rms(acc_scr[...], gp_ref[...])


def _ffn(h2, x1, mods, g_post2, w_ffn_in, w_ffn_out, *, tm, tf, rows_per_batch, per_row):
    n = x1.shape[0]
    nf = D_FF // tf
    if per_row:
        mod_spec = pl.BlockSpec((tm // rows_per_batch, D_MODEL), lambda i, j: (i, 5))
    else:
        bpb = rows_per_batch // tm
        mod_spec = pl.BlockSpec((1, 1, D_MODEL), lambda i, j: (i // bpb, 0, 5))
    row_spec = pl.BlockSpec((tm, D_MODEL), lambda i, j: (i, 0))
    return pl.pallas_call(
        functools.partial(_ffn_kernel, rep=rows_per_batch if per_row else 1),
        grid=(n // tm, nf),
        in_specs=[row_spec,
                  pl.BlockSpec((D_MODEL, tf), lambda i, j: (0, j)),
                  pl.BlockSpec((D_MODEL, tf), lambda i, j: (0, j + nf)),
                  pl.BlockSpec((tf, D_MODEL), lambda i, j: (j, 0)),
                  row_spec, mod_spec,
                  pl.BlockSpec((1, D_MODEL), lambda i, j: (0, 0))],
        out_specs=row_spec,
        out_shape=jax.ShapeDtypeStruct((n, D_MODEL), F32),
        scratch_shapes=[pltpu.VMEM((tm, D_MODEL), F32)],
        compiler_params=_params(2),
        name="ffn_sample" if per_row else "ffn_prompt",
    )(h2, w_ffn_in, w_ffn_in, w_ffn_out, x1, mods, g_post2)


def _w_chunk_kernel(tab_ref, *refs):
    del tab_ref
    f_ref, o_ref, of_ref = refs[-3:]
    for pos, w_ref in enumerate(refs[:-3]):
        o_ref[:, pos * LANES:(pos + 1) * LANES] = w_ref[...].T.astype(BF16)

    @pl.when(pl.program_id(0) == 0)
    def _():
        ft = f_ref[...].T
        lane = lax.broadcasted_iota(jnp.int32, ft.shape, 1)
        of_ref[...] = jnp.where(lane < N_HEADS, ft, 0.0).astype(BF16)


def _chunked_in_weights(w_in):
    d = w_in.shape[0]
    offs = np.cumsum([0, D_ATTN, D_ATTN, D_ATTN, N_HEADS, D_CONV, D_CONV, D_CONV,
                      D_MODEL, D_MODEL])
    q, k, v, f, u, bg, cg, ga, gb = [int(o) for o in offs[:9]]
    cw = D_ATTN // N_CHUNK
    gw = D_MODEL // N_CHUNK
    starts = []
    for j in range(N_CHUNK):
        starts += [u + cw * j, cg + cw * j, bg + cw * j]
        starts += [ga + gw * j + s for s in range(0, gw, LANES)]
        starts += [gb + gw * j + s for s in range(0, gw, LANES)]
        starts += [q + cw * j, k + cw * j, v + cw * j]
    n_slab = CHUNK_W // LANES
    slab = lambda pos: pl.BlockSpec((pl.Element(LANES), pl.Element(d)),
                                    lambda j, tab: (pl.multiple_of(tab[j * n_slab + pos], 8), 0))
    f_slab = pl.BlockSpec((pl.Element(LANES), pl.Element(d)), lambda j, tab: (f, 0))
    return pl.pallas_call(
        _w_chunk_kernel,
        grid_spec=pltpu.PrefetchScalarGridSpec(
            num_scalar_prefetch=1, grid=(N_CHUNK,),
            in_specs=[slab(pos) for pos in range(n_slab)] + [f_slab],
            out_specs=[pl.BlockSpec((d, CHUNK_W), lambda j, tab: (0, j)),
                       pl.BlockSpec((d, LANES), lambda j, tab: (0, 0))]),
        out_shape=[jax.ShapeDtypeStruct((d, N_CHUNK * CHUNK_W), BF16),
                   jax.ShapeDtypeStruct((d, LANES), BF16)],
        compiler_params=_params(1),
        name="w_chunk",
    )(jnp.asarray(starts, jnp.int32), *([w_in.T] * (n_slab + 1)))


def kernel(x_prompt, x_sample, cache_k, cache_v, cache_logf, state_conv, page_table,
           c_prompt, c_sample, w_ada, b_ada, g_pre1, w_in, b_f, w_conv, w_oa, w_ob, w_o,
           g_post1, g_pre2, w_ffn_in, w_ffn_out, g_post2):
    depth = w_in.shape[0]
    assert depth == 1
    nb, t, d = x_prompt.shape
    db, ts, _ = x_sample.shape
    l = 0

    w_main, w_f = _chunked_in_weights(w_in[l])
    b_f_pad = jnp.pad(b_f[l], (0, LANES - N_HEADS)).reshape(1, LANES)
    w_oa_b = w_oa[l].astype(BF16)
    w_ob_b = w_ob[l].astype(BF16)
    w_o_b = w_o[l].astype(BF16)
    w_fin_b = w_ffn_in[l].astype(BF16)
    w_fout_b = w_ffn_out[l].astype(BF16)
    row2 = lambda a: a.reshape(1, -1)

    mods = _adaln(jnp.concatenate([c_prompt, c_sample], axis=0), w_ada[l], b_ada[l])
    mods_p = mods[:nb].reshape(nb, 1, 6 * d)
    mods_s = mods[nb:]

    xp = x_prompt.reshape(nb * t, d)
    (q_hp, kt, vt, k_hp, vt_b, zc, yb, sga, sgb, lft) = _in_proj(
        xp, row2(g_pre1[l]), mods_p, mods_p, w_main, w_f, b_f_pad, w_conv[l], None,
        tm=1024, rows_per_batch=t)
    cumt = _cumsum(lft, nb, t)
    o_hp = _prompt_attn(q_hp, k_hp, vt_b, cumt, nb, t, 512, 512)
    x1, h2 = _merge(o_hp, yb, sga, sgb, xp, mods_p, row2(g_post1[l]), row2(g_pre2[l]),
                    w_oa_b, w_ob_b, w_o_b, tm=512, rows_per_batch=t, per_row=False)
    yp = _ffn(h2, x1, mods_p, row2(g_post2[l]), w_fin_b, w_fout_b, tm=512, tf=512,
              rows_per_batch=t, per_row=False)

    ns = db * ts
    xs = x_sample.reshape(ns, d)
    st = state_conv[l].astype(F32)
    state = (jnp.repeat(st[:, 0, :], ts, axis=0), jnp.repeat(st[:, 1, :], ts, axis=0))
    (qs_hp, kst, vst, _, _, zcs, ybs, sgas, sgbs, lfst) = _in_proj(
        xs, row2(g_pre1[l]), mods_s, mods_s, w_main, w_f, b_f_pad, w_conv[l], state,
        tm=ns, rows_per_batch=ts)
    ks_f = kst[0].T
    vs_f = vst[0].T
    lfs = lfst[0].reshape(N_HEADS, db, ts)
    pad8 = lambda a: jnp.pad(a.reshape(db, ts, -1), ((0, 0), (0, 8 - ts), (0, 0)))
    qs = qs_hp.transpose(1, 0, 2).reshape(ns, D_ATTN).astype(F32)
    lfn_t = jnp.pad(lfs.transpose(1, 0, 2), ((0, 0), (0, 0), (0, LANES - ts)))
    os8 = _sample_attn(page_table, pad8(qs), pad8(ks_f), pad8(vs_f), lfn_t,
                       cache_k[l].transpose(0, 2, 3, 1), cache_v[l].transpose(0, 2, 3, 1),
                       cache_logf[l].transpose(0, 2, 1), ts)
    os_hp = (os8[:, :ts, :].reshape(ns, N_HP, LANES).transpose(1, 0, 2).astype(BF16))
    x1s, h2s = _merge(os_hp, ybs, sgas, sgbs, xs, mods_s, row2(g_post1[l]), row2(g_pre2[l]),
                      w_oa_b, w_ob_b, w_o_b, tm=ns, rows_per_batch=ts, per_row=True)
    ys = _ffn(h2s, x1s, mods_s, row2(g_post2[l]), w_fin_b, w_fout_b, tm=ns, tf=512,
              rows_per_batch=ts, per_row=True)

    heads = lambda a, b_, t_: a.reshape(1, b_, t_, N_HEADS, HEAD_DIM)
    heads_t = lambda a: a.reshape(nb, N_HEADS, HEAD_DIM, t).transpose(0, 3, 1, 2)[None]
    return (yp.reshape(nb, t, d), ys.reshape(db, ts, d),
            heads_t(kt), heads_t(vt), lft.transpose(0, 2, 1)[None],
            zc.reshape(nb, t, D_CONV)[:, t - (CONV_W - 1):, :][None],
            heads(ks_f, db, ts), heads(vs_f, db, ts), lfs.transpose(1, 2, 0)[None],
            zcs.reshape(db, ts, D_CONV)[:, ts - (CONV_W - 1):, :][None])
```

```python
import functools

import jax
import jax.numpy as jnp
import numpy as np
from jax import lax
from jax.experimental import pallas as pl
from jax.experimental.pallas import tpu as pltpu

F32 = jnp.float32
BF16 = jnp.bfloat16

D_MODEL = 2048
N_HEADS = 16
HEAD_DIM = 64
D_ATTN = N_HEADS * HEAD_DIM
D_CONV = 1024
CONV_W = 3
D_FF = 5632
RMS_EPS = 1e-6
ATTN_SCALE = HEAD_DIM ** -0.5
LOG2E = 1.4426950408889634
PAGE_SIZE = 128

LANES = 128
N_HP = D_ATTN // LANES
N_CHUNK = 8
CHUNK_W = 6 * (D_ATTN // N_CHUNK) + 2 * (D_MODEL // N_CHUNK)
VMEM_LIMIT = 56 * 1024 * 1024
FFN_VMEM_LIMIT = 60 * 1024 * 1024

def _params(n_axes, vmem=VMEM_LIMIT, flags=None):
    return pltpu.CompilerParams(
        dimension_semantics=("arbitrary",) * n_axes, vmem_limit_bytes=vmem, flags=flags)


def _rms(x, g):
    return x * lax.rsqrt(jnp.mean(x * x, axis=-1, keepdims=True) + RMS_EPS) * g


def _repeat_rows(m, rep):
    n = m.shape[0]
    assert rep & (rep - 1) == 0
    src = lax.broadcasted_iota(jnp.int32, (n * rep, n), 0) >> (rep.bit_length() - 1)
    col = lax.broadcasted_iota(jnp.int32, (n * rep, n), 1)
    onehot = jnp.where(src == col, 1.0, 0.0).astype(BF16)
    out = None
    for _ in range(3):
        piece = m.astype(BF16)
        part = jnp.dot(onehot, piece, preferred_element_type=F32)
        out = part if out is None else out + part
        m = m - piece.astype(F32)
    return out


def _mod(ref, rep=1):
    if len(ref.shape) == 3:
        return ref[0]
    return ref[...] if rep == 1 else _repeat_rows(ref[...], rep)


def _log_sigmoid(z):
    return jnp.minimum(z, 0.0) - jnp.log1p(jnp.exp(-jnp.abs(z)))


def _adaln_kernel(c_ref, w_ref, b_ref, o_ref):
    @pl.when(pl.program_id(0) == 0)
    def _():
        o_ref[...] = jnp.broadcast_to(b_ref[...], o_ref.shape)

    c = c_ref[...]
    s = (c * jax.nn.sigmoid(c)).astype(BF16)
    o_ref[...] += jnp.dot(s, w_ref[...].astype(BF16), preferred_element_type=F32)


def _adaln(c, w_ada, b_ada):
    m, d = c.shape
    n = w_ada.shape[1]
    tk = LANES
    return pl.pallas_call(
        _adaln_kernel,
        grid=(d // tk,),
        in_specs=[pl.BlockSpec((m, tk), lambda k: (0, k)),
                  pl.BlockSpec((tk, n), lambda k: (k, 0)),
                  pl.BlockSpec((1, n), lambda k: (0, 0))],
        out_specs=pl.BlockSpec((m, n), lambda k: (0, 0)),
        out_shape=jax.ShapeDtypeStruct((m, n), F32),
        compiler_params=_params(1),
        name="adaln",
    )(c, w_ada, b_ada.reshape(1, n))


def _in_proj_kernel(*refs, tm, blocks_per_batch, per_row, rep, q_scale):
    if per_row:
        (x_ref, g_ref, sh_ref, sc_ref, w_ref, wf_ref, bf_ref, wc_ref, s0_ref, s1_ref,
         q_ref, kt_ref, vt_ref, kb_ref, vtb_ref, zc_ref, yb_ref, sga_ref, sgb_ref,
         lft_ref, h_scr) = refs
    else:
        (x_ref, g_ref, sh_ref, sc_ref, w_ref, wf_ref, bf_ref, wc_ref,
         q_ref, kt_ref, vt_ref, kb_ref, vtb_ref, zc_ref, yb_ref, sga_ref, sgb_ref,
         lft_ref, h_scr, halo_scr) = refs
    i = pl.program_id(0)
    j = pl.program_id(1)

    @pl.when(j == 0)
    def _():
        h = _rms(x_ref[...], g_ref[...]) * (1.0 + _mod(sc_ref, rep)) + _mod(sh_ref, rep)
        hb = h.astype(BF16)
        h_scr[...] = hb
        f = jnp.dot(hb, wf_ref[...], preferred_element_type=F32)
        lf = _log_sigmoid(f + bf_ref[...])
        lft_ref[0] = lf.T[0:N_HEADS, :]

    res = jnp.dot(h_scr[...], w_ref[...], preferred_element_type=F32)
    cw = D_ATTN // N_CHUNK
    gw = D_MODEL // N_CHUNK
    u = res[:, 0:cw]
    cg = res[:, cw:2 * cw]
    bg = res[:, 2 * cw:3 * cw]
    ga = res[:, 3 * cw:3 * cw + gw]
    gb = res[:, 3 * cw + gw:3 * cw + 2 * gw]
    o = 3 * cw + 2 * gw
    q = res[:, o:o + cw]
    k = res[:, o + cw:o + 2 * cw]
    v = res[:, o + 2 * cw:o + 3 * cw]

    q_ref[0] = (q * q_scale).astype(BF16)
    kt_ref[0] = k.T
    vt = v.T
    vt_ref[0] = vt
    vtb_ref[0] = vt.astype(BF16)
    kb_ref[0] = k.astype(BF16)
    sga_ref[...] = jax.nn.sigmoid(ga).astype(BF16)
    sgb_ref[...] = jax.nn.sigmoid(gb).astype(BF16)

    zc = cg * u
    zc_ref[...] = zc
    row = lax.broadcasted_iota(jnp.int32, zc.shape, 0)
    r1 = pltpu.roll(zc, 1, 0)
    r2 = pltpu.roll(zc, 2, 0)
    if per_row:
        t = row & 3
        s0 = s0_ref[...]
        s1 = s1_ref[...]
        p1 = jnp.where(t >= 1, r1, s1)
        p2 = jnp.where(t >= 2, r2, jnp.where(t == 1, s1, s0))
    else:
        first = (i % blocks_per_batch) == 0
        halo = halo_scr[j]
        h6 = jnp.where(first, 0.0, halo[6:7, :])
        h7 = jnp.where(first, 0.0, halo[7:8, :])
        p1 = jnp.where(row == 0, h7, r1)
        p2 = jnp.where(row == 0, h6, jnp.where(row == 1, h7, r2))
        halo_scr[j] = zc[tm - 8:tm, :]
    wc = wc_ref[...]
    y = p2 * wc[0:1, :]
    y = y + p1 * wc[1:2, :]
    y = y + zc * wc[2:3, :]
    yb_ref[...] = (bg * y).astype(BF16)


def _in_proj(x, g_pre, shift, scale, w_main, w_f, b_f, w_conv, state, *, tm, rows_per_batch):
    n = x.shape[0]
    per_row = state is not None
    nblk = n // tm
    cw = D_ATTN // N_CHUNK
    gw = D_MODEL // N_CHUNK
    if per_row:
        bpb = 1
        mod_spec = lambda c: pl.BlockSpec((tm // rows_per_batch, D_MODEL), lambda i, j: (i, c))
    else:
        bpb = rows_per_batch // tm
        mod_spec = lambda c: pl.BlockSpec((1, 1, D_MODEL), lambda i, j: (i // bpb, 0, c))
    in_specs = [
        pl.BlockSpec((tm, D_MODEL), lambda i, j: (i, 0)),
        pl.BlockSpec((1, D_MODEL), lambda i, j: (0, 0)),
        mod_spec(0), mod_spec(1),
        pl.BlockSpec((D_MODEL, CHUNK_W), lambda i, j: (0, j)),
        pl.BlockSpec((D_MODEL, LANES), lambda i, j: (0, 0)),
        pl.BlockSpec((1, LANES), lambda i, j: (0, 0)),
        pl.BlockSpec((CONV_W, cw), lambda i, j: (0, j)),
    ]
    args = [x, g_pre, shift, scale, w_main, w_f, b_f, w_conv]
    scratch = [pltpu.VMEM((tm, D_MODEL), BF16)]
    if per_row:
        in_specs += [pl.BlockSpec((tm, cw), lambda i, j: (i, j))] * 2
        args += [state[0], state[1]]
    else:
        scratch.append(pltpu.VMEM((N_CHUNK, 8, cw), F32))
    hp_spec = pl.BlockSpec((1, tm, LANES), lambda i, j: (j, i, 0))
    col_spec = lambda w: pl.BlockSpec((tm, w), lambda i, j: (i, j))
    nbat = nblk // bpb
    t_spec = lambda rows: pl.BlockSpec(
        (1, rows, tm), lambda i, j: (i // bpb, j if rows == LANES else 0, i % bpb))
    t_shape = lambda rows, dt=F32: jax.ShapeDtypeStruct((nbat, rows, bpb * tm), dt)
    out_specs = [hp_spec, t_spec(LANES), t_spec(LANES), hp_spec, t_spec(LANES),
                 col_spec(cw), col_spec(cw), col_spec(gw), col_spec(gw),
                 t_spec(N_HEADS)]
    hp_shape = jax.ShapeDtypeStruct((N_HP, n, LANES), BF16)
    out_shape = [hp_shape, t_shape(D_ATTN), t_shape(D_ATTN), hp_shape, t_shape(D_ATTN, BF16),
                 jax.ShapeDtypeStruct((n, D_CONV), F32), jax.ShapeDtypeStruct((n, D_CONV), BF16),
                 jax.ShapeDtypeStruct((n, D_MODEL), BF16), jax.ShapeDtypeStruct((n, D_MODEL), BF16),
                 t_shape(N_HEADS)]
    return pl.pallas_call(
        functools.partial(_in_proj_kernel, tm=tm, blocks_per_batch=bpb, per_row=per_row,
                          rep=rows_per_batch if per_row else 1,
                          q_scale=ATTN_SCALE if per_row else ATTN_SCALE * LOG2E),
        grid=(nblk, N_CHUNK),
        in_specs=in_specs, out_specs=out_specs, out_shape=out_shape,
        scratch_shapes=scratch,
        compiler_params=_params(2),
        name="in_proj_sample" if per_row else "in_proj_prompt",
    )(*args)


def _lane_cumsum(x):
    n = x.shape[-1]
    lane = lax.broadcasted_iota(jnp.int32, x.shape, x.ndim - 1)
    s = 1
    while s < n:
        x = x + jnp.where(lane >= s, pltpu.roll(x, s, x.ndim - 1), 0.0)
        s *= 2
    return x


def _cumsum_kernel(lft_ref, cumt_ref):
    cumt_ref[0] = _lane_cumsum(lft_ref[0]) * LOG2E


def _cumsum(lft, n_batch, t):
    spec = pl.BlockSpec((1, N_HEADS, t), lambda b: (b, 0, 0))
    return pl.pallas_call(
        _cumsum_kernel,
        grid=(n_batch,),
        in_specs=[spec], out_specs=spec,
        out_shape=jax.ShapeDtypeStruct((n_batch, N_HEADS, t), F32),
        compiler_params=_params(1),
        name="cumsum",
    )(lft)


ACC_ROWS = HEAD_DIM + 16


def _prompt_attn_kernel(qi_ref, kj_ref, q_ref, k_ref, vt_ref, cq_ref, ck_ref, o_ref,
                        acc_scr, m_scr, lg_scr, st_scr, mask_scr, *, tq, tk):
    t = pl.program_id(1)
    qi = qi_ref[t]
    kj = kj_ref[t]
    assert tq == tk

    @pl.when(kj == 0)
    def _():
        acc_scr[...] = jnp.zeros_like(acc_scr)
        m_scr[...] = jnp.full_like(m_scr, -jnp.inf)

    lane = lax.broadcasted_iota(jnp.int32, (1, LANES), 1)
    lo_half = lane < HEAD_DIM
    ones_rows = jnp.ones((ACC_ROWS - HEAD_DIM, tk), BF16)
    row8 = lax.broadcasted_iota(jnp.int32, (8, tq), 0)

    def logits(hp, slot, masked):
        q2 = q_ref[hp]
        k2 = k_ref[hp]
        m_pair = m_scr[hp]
        m_out = []
        stats = []
        for e in range(2):
            head = 2 * hp + e
            sel = lo_half if e == 0 else jnp.logical_not(lo_half)
            km = jnp.where(sel, k2, jnp.zeros_like(k2))
            st = lax.dot_general(km, q2, (((1,), (1,)), ((), ())),
                                 preferred_element_type=F32)
            ck_col = jnp.broadcast_to(ck_ref[0, head:head + 1, :], (LANES, tk)).T
            lg = st - pltpu.repeat(ck_col, tq // LANES, axis=1)
            if masked:
                lg = lg + mask_scr[...]
            cq = cq_ref[0, head:head + 1, :]
            m_prev = m_pair[e:e + 1, :]
            m_new = jnp.maximum(m_prev, jnp.max(lg, axis=0, keepdims=True) + cq)
            lg_scr[slot, e] = lg
            stats += [cq - m_new, jnp.exp2(m_prev - m_new)]
            m_out.append(m_new)
        st_scr[slot] = jnp.where(row8 == 0, stats[0], jnp.where(
            row8 == 1, stats[1], jnp.where(row8 == 2, stats[2], stats[3])))
        m_scr[hp] = jnp.where(row8 == 0, m_out[0], m_out[1])

    def update(hp, slot, last):
        vt_pair = vt_ref[0, hp * LANES:(hp + 1) * LANES, :]
        acc_pair = acc_scr[hp]
        stat = st_scr[slot]
        acc_out = []
        for e in range(2):
            p = jnp.exp2(lg_scr[slot, e] + stat[2 * e:2 * e + 1, :]).astype(BF16)
            vaug = jnp.concatenate(
                [vt_pair[e * HEAD_DIM:(e + 1) * HEAD_DIM, :], ones_rows], axis=0)
            acc_out.append(acc_pair[e * ACC_ROWS:(e + 1) * ACC_ROWS, :]
                           * stat[2 * e + 1:2 * e + 2, :]
                           + jnp.dot(vaug, p, preferred_element_type=F32))
        if last:
            halves = [a[0:HEAD_DIM, :] * (1.0 / a[HEAD_DIM:HEAD_DIM + 1, :]) for a in acc_out]
            o_ref[hp] = jnp.concatenate(halves, axis=0).T.astype(BF16)
        else:
            acc_scr[hp] = jnp.concatenate(acc_out, axis=0)

    def all_pairs(diagonal):
        logits(0, 0, diagonal)
        for hp in range(N_HP):
            if hp + 1 < N_HP:
                logits(hp + 1, (hp + 1) % 2, diagonal)
            update(hp, hp % 2, diagonal)

    @pl.when(kj < qi)
    def _():
        all_pairs(False)

    @pl.when(kj == qi)
    def _():
        key = lax.broadcasted_iota(jnp.int32, (tk, tq), 0)
        qry = lax.broadcasted_iota(jnp.int32, (tk, tq), 1)
        mask_scr[...] = jnp.where(key <= qry, 0.0, -jnp.inf)
        all_pairs(True)


def _prompt_attn(q_hp, k_hp, vt_b, cumt, n_batch, t, tq, tk):
    nq = t // tq
    nk = t // tk
    pairs = [(a, b) for a in range(nq) for b in range((a + 1) * tq // tk)]
    qi_tab = jnp.asarray([p[0] for p in pairs], jnp.int32)
    kj_tab = jnp.asarray([p[1] for p in pairs], jnp.int32)
    qblk = (N_HP, tq, LANES)
    grid_spec = pltpu.PrefetchScalarGridSpec(
        num_scalar_prefetch=2,
        grid=(n_batch, len(pairs)),
        in_specs=[
            pl.BlockSpec(qblk, lambda b, s, qi, kj: (0, b * nq + qi[s], 0)),
            pl.BlockSpec((N_HP, tk, LANES), lambda b, s, qi, kj: (0, b * nk + kj[s], 0)),
            pl.BlockSpec((1, D_ATTN, tk), lambda b, s, qi, kj: (b, 0, kj[s])),
            pl.BlockSpec((1, N_HEADS, tq), lambda b, s, qi, kj: (b, 0, qi[s])),
            pl.BlockSpec((1, N_HEADS, tk), lambda b, s, qi, kj: (b, 0, kj[s])),
        ],
        out_specs=pl.BlockSpec(qblk, lambda b, s, qi, kj: (0, b * nq + qi[s], 0)),
        scratch_shapes=[pltpu.VMEM((N_HP, 2 * ACC_ROWS, tq), F32),
                        pltpu.VMEM((N_HP, 8, tq), F32),
                        pltpu.VMEM((2, 2, tk, tq), F32),
                        pltpu.VMEM((2, 8, tq), F32),
                        pltpu.VMEM((tk, tq), F32)],
    )
    return pl.pallas_call(
        functools.partial(_prompt_attn_kernel, tq=tq, tk=tk),
        grid_spec=grid_spec,
        out_shape=jax.ShapeDtypeStruct((N_HP, n_batch * t, LANES), BF16),
        compiler_params=_params(2),
        name="prompt_attn",
    )(qi_tab, kj_tab, q_hp, k_hp, vt_b, cumt, cumt)


def _sample_attn_kernel(*refs, n_pages, n_new):
    pt_ref = refs[0]
    q_ref, kn_ref, vn_ref, lfn_ref = refs[1:5]
    k_refs = refs[5:5 + n_pages]
    v_refs = refs[5 + n_pages:5 + 2 * n_pages]
    lf_refs = refs[5 + 2 * n_pages:5 + 3 * n_pages]
    o_ref = refs[5 + 3 * n_pages]
    del pt_ref
    nr = n_new * N_HEADS
    nt = ((1,), (1,)), ((), ())
    page_t = lambda ref: ref[0].reshape(D_ATTN, PAGE_SIZE).astype(BF16)

    q = q_ref[0]
    hrow = lax.broadcasted_iota(jnp.int32, (N_HEADS, D_ATTN), 0)
    hlane = lax.broadcasted_iota(jnp.int32, (N_HEADS, D_ATTN), 1) // HEAD_DIM
    hmask = hrow == hlane
    qbd = jnp.concatenate(
        [jnp.where(hmask, jnp.broadcast_to(q[qq:qq + 1, :], (N_HEADS, D_ATTN)), 0.0)
         for qq in range(n_new)], axis=0).astype(BF16)

    s_parts = []
    lf_parts = []
    for c in range(n_pages):
        s_parts.append(jnp.dot(qbd, page_t(k_refs[c]), preferred_element_type=F32))
        lf_parts.append(lf_refs[c][0])
    s_past = jnp.concatenate(s_parts, axis=1)
    ck_past = _lane_cumsum(jnp.concatenate(lf_parts, axis=1))
    n_past = ck_past.shape[1]
    c_total = ck_past[:, n_past - 1:n_past]

    kn = jnp.concatenate(
        [kn_ref[0], jnp.zeros((PAGE_SIZE - 8, D_ATTN), F32)], axis=0).astype(BF16)
    vn = jnp.concatenate(
        [vn_ref[0], jnp.zeros((PAGE_SIZE - 8, D_ATTN), F32)], axis=0).astype(BF16)
    s_new = lax.dot_general(qbd, kn, nt, preferred_element_type=F32)
    ck_new = c_total + _lane_cumsum(lfn_ref[0])

    tile = lambda a: jnp.concatenate([a] * n_new, axis=0)
    ck_past_r = tile(ck_past)
    ck_new_r = tile(ck_new)
    qq_row = lax.broadcasted_iota(jnp.int32, (nr, PAGE_SIZE), 0) // N_HEADS
    lane = lax.broadcasted_iota(jnp.int32, (nr, PAGE_SIZE), 1)
    cq = jnp.sum(jnp.where(lane == qq_row, ck_new_r, 0.0), axis=1, keepdims=True)

    lg_past = s_past + (cq - ck_past_r)
    lg_new = jnp.where(lane <= qq_row, s_new + (cq - ck_new_r), -jnp.inf)
    m = jnp.maximum(jnp.max(lg_past, axis=1, keepdims=True),
                    jnp.max(lg_new, axis=1, keepdims=True))
    p_past = jnp.exp(lg_past - m)
    p_new = jnp.exp(lg_new - m)
    denom = jnp.sum(p_past, axis=1, keepdims=True) + jnp.sum(p_new, axis=1, keepdims=True)
    acc = jnp.dot(p_new.astype(BF16), vn, preferred_element_type=F32)
    pb = p_past.astype(BF16)
    for c in range(n_pages):
        acc = acc + lax.dot_general(pb[:, c * PAGE_SIZE:(c + 1) * PAGE_SIZE], page_t(v_refs[c]),
                                    nt, preferred_element_type=F32)
    acc = acc / denom
    orow = lax.broadcasted_iota(jnp.int32, (8, D_ATTN), 0)
    out = jnp.zeros((8, D_ATTN), F32)
    for qq in range(n_new):
        blk = jnp.where(hmask, acc[qq * N_HEADS:(qq + 1) * N_HEADS, :], 0.0)
        out = jnp.where(orow == qq, jnp.sum(blk, axis=0, keepdims=True), out)
    o_ref[0] = out


def _sample_attn(page_table, q8, kn8, vn8, lfn_t, cache_kt, cache_vt, cache_lft, n_new):
    n_batch, n_pages = page_table.shape
    row_spec = lambda w: pl.BlockSpec((1, 8, w), lambda b, pt: (b, 0, 0))
    kv_page = lambda c: pl.BlockSpec((1, N_HEADS, HEAD_DIM, PAGE_SIZE),
                                     lambda b, pt: (pt[b * n_pages + c], 0, 0, 0))
    lf_page = lambda c: pl.BlockSpec((1, N_HEADS, PAGE_SIZE),
                                     lambda b, pt: (pt[b * n_pages + c], 0, 0))
    in_specs = [row_spec(D_ATTN), row_spec(D_ATTN), row_spec(D_ATTN),
                pl.BlockSpec((1, N_HEADS, LANES), lambda b, pt: (b, 0, 0))]
    in_specs += [kv_page(c) for c in range(n_pages)]
    in_specs += [kv_page(c) for c in range(n_pages)]
    in_specs += [lf_page(c) for c in range(n_pages)]
    grid_spec = pltpu.PrefetchScalarGridSpec(
        num_scalar_prefetch=1,
        grid=(n_batch,),
        in_specs=in_specs,
        out_specs=row_spec(D_ATTN),
    )
    return pl.pallas_call(
        functools.partial(_sample_attn_kernel, n_pages=n_pages, n_new=n_new),
        grid_spec=grid_spec,
        out_shape=jax.ShapeDtypeStruct((n_batch, 8, D_ATTN), F32),
        compiler_params=_params(1),
        name="sample_attn",
    )(page_table.reshape(-1), q8, kn8, vn8, lfn_t,
      *([cache_kt] * n_pages), *([cache_vt] * n_pages), *([cache_lft] * n_pages))


def _merge_kernel(o_ref, yb_ref, sga_ref, sgb_ref, x_ref, gt_ref, sh_ref, sc_ref,
                  gp1_ref, gp2_ref, woa_ref, wob_ref, wo_ref, x1_ref, h2_ref, *, rep):
    tm = x_ref.shape[0]
    gate, shift, scale = _mod(gt_ref, rep), _mod(sh_ref, rep), _mod(sc_ref, rep)
    n_split = 2 if tm % 512 == 0 else 1
    for r in range(n_split):
        rows = slice(r * tm // n_split, (r + 1) * tm // n_split)
        sub = lambda v: v if v.shape[0] == 1 else v[rows, :]
        o = jnp.concatenate([o_ref[hp, rows, :] for hp in range(N_HP)], axis=1)
        a = jnp.dot(o, woa_ref[...], preferred_element_type=F32)
        b = jnp.dot(yb_ref[rows, :], wob_ref[...], preferred_element_type=F32)
        mm = (sga_ref[rows, :].astype(F32) * a + sgb_ref[rows, :].astype(F32) * b).astype(BF16)
        m = jnp.dot(mm, wo_ref[...], preferred_element_type=F32)
        x1 = x_ref[rows, :] + sub(gate) * _rms(m, gp1_ref[...])
        x1_ref[rows, :] = x1
        h2 = _rms(x1, gp2_ref[...]) * (1.0 + sub(scale)) + sub(shift)
        h2_ref[rows, :] = h2.astype(BF16)


def _merge(o_hp, yb, sga, sgb, x, mods, g_post1, g_pre2, w_oa, w_ob, w_o, *, tm,
           rows_per_batch, per_row):
    n = x.shape[0]
    if per_row:
        mod_spec = lambda c: pl.BlockSpec((tm // rows_per_batch, D_MODEL), lambda i: (i, c))
    else:
        bpb = rows_per_batch // tm
        mod_spec = lambda c: pl.BlockSpec((1, 1, D_MODEL), lambda i: (i // bpb, 0, c))
    row_spec = lambda w: pl.BlockSpec((tm, w), lambda i: (i, 0))
    const = lambda shape: pl.BlockSpec(shape, lambda i: (0,) * len(shape),
                                       pipeline_mode=pl.Buffered(1))
    return pl.pallas_call(
        functools.partial(_merge_kernel, rep=rows_per_batch if per_row else 1),
        grid=(n // tm,),
        in_specs=[pl.BlockSpec((N_HP, tm, LANES), lambda i: (0, i, 0)),
                  row_spec(D_CONV), row_spec(D_MODEL), row_spec(D_MODEL), row_spec(D_MODEL),
                  mod_spec(2), mod_spec(3), mod_spec(4),
                  const((1, D_MODEL)), const((1, D_MODEL)),
                  const((D_ATTN, D_MODEL)), const((D_CONV, D_MODEL)),
                  const((D_MODEL, D_MODEL))],
        out_specs=[row_spec(D_MODEL), row_spec(D_MODEL)],
        out_shape=[jax.ShapeDtypeStruct((n, D_MODEL), F32),
                   jax.ShapeDtypeStruct((n, D_MODEL), BF16)],
        compiler_params=_params(1),
        name="merge_sample" if per_row else "merge_prompt",
    )(o_hp, yb, sga, sgb, x, mods, mods, mods, g_post1, g_pre2, w_oa, w_ob, w_o)


def _ffn_kernel(h_ref, wg_ref, wu_ref, wout_ref, x1_ref, gt_ref, gp_ref, y_ref, *, rep):
    j = pl.program_id(1)

    @pl.when(j == 0)
    def _():
        y_ref[...] = jnp.zeros_like(y_ref)

    h = h_ref[...]
    g = jnp.dot(h, wg_ref[...], preferred_element_type=F32)
    u = jnp.dot(h, wu_ref[...], preferred_element_type=F32)
    act = (g * jax.nn.sigmoid(g) * u).astype(BF16)
    y_ref[...] += jnp.dot(act, wout_ref[...], preferred_element_type=F32)

    @pl.when(j == pl.num_programs(1) - 1)
    def _():
        y_ref[...] = x1_ref[...] + _mod(gt_ref, rep) * _rms(y_ref[...], gp_ref[...])


def _ffn(h2, x1, mods, g_post2, w_ffn_in, w_ffn_out, *, tm, tf, rows_per_batch, per_row):
    n = x1.shape[0]
    nf = D_FF // tf
    if per_row:
        mod_spec = pl.BlockSpec((tm // rows_per_batch, D_MODEL), lambda i, j: (i, 5))
    else:
        bpb = rows_per_batch // tm
        mod_spec = pl.BlockSpec((1, 1, D_MODEL), lambda i, j: (i // bpb, 0, 5))
    row_spec = pl.BlockSpec((tm, D_MODEL), lambda i, j: (i, 0))
    x1_spec = pl.BlockSpec((tm, D_MODEL), lambda i, j: (i, 0), pipeline_mode=pl.Buffered(1))
    return pl.pallas_call(
        functools.partial(_ffn_kernel, rep=rows_per_batch if per_row else 1),
        grid=(n // tm, nf),
        in_specs=[row_spec,
                  pl.BlockSpec((D_MODEL, tf), lambda i, j: (0, j)),
                  pl.BlockSpec((D_MODEL, tf), lambda i, j: (0, j + nf)),
                  pl.BlockSpec((tf, D_MODEL), lambda i, j: (j, 0)),
                  x1_spec, mod_spec,
                  pl.BlockSpec((1, D_MODEL), lambda i, j: (0, 0))],
        out_specs=row_spec,
        out_shape=jax.ShapeDtypeStruct((n, D_MODEL), F32),
        compiler_params=_params(2, vmem=FFN_VMEM_LIMIT),
        name="ffn_sample" if per_row else "ffn_prompt",
    )(h2, w_ffn_in, w_ffn_in, w_ffn_out, x1, mods, g_post2)


def _w_chunk_kernel(tab_ref, *refs):
    del tab_ref
    f_ref, o_ref, of_ref = refs[-3:]
    for pos, w_ref in enumerate(refs[:-3]):
        o_ref[:, pos * LANES:(pos + 1) * LANES] = w_ref[...].T.astype(BF16)

    @pl.when(pl.program_id(0) == 0)
    def _():
        ft = f_ref[...].T
        lane = lax.broadcasted_iota(jnp.int32, ft.shape, 1)
        of_ref[...] = jnp.where(lane < N_HEADS, ft, 0.0).astype(BF16)


def _chunked_in_weights(w_in):
    d = w_in.shape[0]
    offs = np.cumsum([0, D_ATTN, D_ATTN, D_ATTN, N_HEADS, D_CONV, D_CONV, D_CONV,
                      D_MODEL, D_MODEL])
    q, k, v, f, u, bg, cg, ga, gb = [int(o) for o in offs[:9]]
    cw = D_ATTN // N_CHUNK
    gw = D_MODEL // N_CHUNK
    starts = []
    for j in range(N_CHUNK):
        starts += [u + cw * j, cg + cw * j, bg + cw * j]
        starts += [ga + gw * j + s for s in range(0, gw, LANES)]
        starts += [gb + gw * j + s for s in range(0, gw, LANES)]
        starts += [q + cw * j, k + cw * j, v + cw * j]
    n_slab = CHUNK_W // LANES
    slab = lambda pos: pl.BlockSpec((pl.Element(LANES), pl.Element(d)),
                                    lambda j, tab: (pl.multiple_of(tab[j * n_slab + pos], 8), 0))
    f_slab = pl.BlockSpec((pl.Element(LANES), pl.Element(d)), lambda j, tab: (f, 0))
    return pl.pallas_call(
        _w_chunk_kernel,
        grid_spec=pltpu.PrefetchScalarGridSpec(
            num_scalar_prefetch=1, grid=(N_CHUNK,),
            in_specs=[slab(pos) for pos in range(n_slab)] + [f_slab],
            out_specs=[pl.BlockSpec((d, CHUNK_W), lambda j, tab: (0, j)),
                       pl.BlockSpec((d, LANES), lambda j, tab: (0, 0))]),
        out_shape=[jax.ShapeDtypeStruct((d, N_CHUNK * CHUNK_W), BF16),
                   jax.ShapeDtypeStruct((d, LANES), BF16)],
        compiler_params=_params(1),
        name="w_chunk",
    )(jnp.asarray(starts, jnp.int32), *([w_in.T] * (n_slab + 1)))


def kernel(x_prompt, x_sample, cache_k, cache_v, cache_logf, state_conv, page_table,
           c_prompt, c_sample, w_ada, b_ada, g_pre1, w_in, b_f, w_conv, w_oa, w_ob, w_o,
           g_post1, g_pre2, w_ffn_in, w_ffn_out, g_post2):
    depth = w_in.shape[0]
    assert depth == 1
    nb, t, d = x_prompt.shape
    db, ts, _ = x_sample.shape
    l = 0

    w_main, w_f = _chunked_in_weights(w_in[l])
    b_f_pad = jnp.pad(b_f[l], (0, LANES - N_HEADS)).reshape(1, LANES)
    w_oa_b = w_oa[l].astype(BF16)
    w_ob_b = w_ob[l].astype(BF16)
    w_o_b = w_o[l].astype(BF16)
    w_fin_b = w_ffn_in[l].astype(BF16)
    w_fout_b = w_ffn_out[l].astype(BF16)
    row2 = lambda a: a.reshape(1, -1)

    mods = _adaln(jnp.concatenate([c_prompt, c_sample], axis=0), w_ada[l], b_ada[l])
    mods_p = mods[:nb].reshape(nb, 1, 6 * d)
    mods_s = mods[nb:]

    xp = x_prompt.reshape(nb * t, d)
    (q_hp, kt, vt, k_hp, vt_b, zc, yb, sga, sgb, lft) = _in_proj(
        xp, row2(g_pre1[l]), mods_p, mods_p, w_main, w_f, b_f_pad, w_conv[l], None,
        tm=1024, rows_per_batch=t)
    cumt = _cumsum(lft, nb, t)
    o_hp = _prompt_attn(q_hp, k_hp, vt_b, cumt, nb, t, 512, 512)
    x1, h2 = _merge(o_hp, yb, sga, sgb, xp, mods_p, row2(g_post1[l]), row2(g_pre2[l]),
                    w_oa_b, w_ob_b, w_o_b, tm=512, rows_per_batch=t, per_row=False)
    yp = _ffn(h2, x1, mods_p, row2(g_post2[l]), w_fin_b, w_fout_b, tm=1024, tf=512,
              rows_per_batch=t, per_row=False)

    ns = db * ts
    xs = x_sample.reshape(ns, d)
    st = state_conv[l].astype(F32)
    state = (jnp.repeat(st[:, 0, :], ts, axis=0), jnp.repeat(st[:, 1, :], ts, axis=0))
    (qs_hp, kst, vst, _, _, zcs, ybs, sgas, sgbs, lfst) = _in_proj(
        xs, row2(g_pre1[l]), mods_s, mods_s, w_main, w_f, b_f_pad, w_conv[l], state,
        tm=ns, rows_per_batch=ts)
    ks_f = kst[0].T
    vs_f = vst[0].T
    lfs = lfst[0].reshape(N_HEADS, db, ts)
    pad8 = lambda a: jnp.pad(a.reshape(db, ts, -1), ((0, 0), (0, 8 - ts), (0, 0)))
    qs = qs_hp.transpose(1, 0, 2).reshape(ns, D_ATTN).astype(F32)
    lfn_t = jnp.pad(lfs.transpose(1, 0, 2), ((0, 0), (0, 0), (0, LANES - ts)))
    os8 = _sample_attn(page_table, pad8(qs), pad8(ks_f), pad8(vs_f), lfn_t,
                       cache_k[l].transpose(0, 2, 3, 1), cache_v[l].transpose(0, 2, 3, 1),
                       cache_logf[l].transpose(0, 2, 1), ts)
    os_hp = (os8[:, :ts, :].reshape(ns, N_HP, LANES).transpose(1, 0, 2).astype(BF16))
    x1s, h2s = _merge(os_hp, ybs, sgas, sgbs, xs, mods_s, row2(g_post1[l]), row2(g_pre2[l]),
                      w_oa_b, w_ob_b, w_o_b, tm=ns, rows_per_batch=ts, per_row=True)
    ys = _ffn(h2s, x1s, mods_s, row2(g_post2[l]), w_fin_b, w_fout_b, tm=ns, tf=512,
              rows_per_batch=ts, per_row=True)

    heads = lambda a, b_, t_: a.reshape(1, b_, t_, N_HEADS, HEAD_DIM)
    heads_t = lambda a: a.reshape(nb, N_HEADS, HEAD_DIM, t).transpose(0, 3, 1, 2)[None]
    return (yp.reshape(nb, t, d), ys.reshape(db, ts, d),
            heads_t(kt), heads_t(vt), lft.transpose(0, 2, 1)[None],
            zc.reshape(nb, t, D_CONV)[:, t - (CONV_W - 1):, :][None],
            heads(ks_f, db, ts), heads(vs_f, db, ts), lfs.transpose(1, 2, 0)[None],
            zcs.reshape(db, ts, D_CONV)[:, ts - (CONV_W - 1):, :][None])
```

```python
import functools

import jax
import jax.numpy as jnp
import numpy as np
from jax import lax
from jax.experimental import pallas as pl
from jax.experimental.pallas import tpu as pltpu

F32 = jnp.float32
BF16 = jnp.bfloat16

D_MODEL = 2048
N_HEADS = 16
HEAD_DIM = 64
D_ATTN = N_HEADS * HEAD_DIM
D_CONV = 1024
CONV_W = 3
D_FF = 5632
RMS_EPS = 1e-6
ATTN_SCALE = HEAD_DIM ** -0.5
LOG2E = 1.4426950408889634
PAGE_SIZE = 128

LANES = 128
N_HP = D_ATTN // LANES
N_CHUNK = 8
CHUNK_W = 6 * (D_ATTN // N_CHUNK) + 2 * (D_MODEL // N_CHUNK)
VMEM_LIMIT = 56 * 1024 * 1024
def _params(n_axes, vmem=VMEM_LIMIT, flags=None):
    return pltpu.CompilerParams(
        dimension_semantics=("arbitrary",) * n_axes, vmem_limit_bytes=vmem, flags=flags)


def _rms(x, g):
    return x * lax.rsqrt(jnp.mean(x * x, axis=-1, keepdims=True) + RMS_EPS) * g


def _repeat_rows(m, rep):
    n = m.shape[0]
    assert rep & (rep - 1) == 0
    src = lax.broadcasted_iota(jnp.int32, (n * rep, n), 0) >> (rep.bit_length() - 1)
    col = lax.broadcasted_iota(jnp.int32, (n * rep, n), 1)
    onehot = jnp.where(src == col, 1.0, 0.0).astype(BF16)
    out = None
    for _ in range(3):
        piece = m.astype(BF16)
        part = jnp.dot(onehot, piece, preferred_element_type=F32)
        out = part if out is None else out + part
        m = m - piece.astype(F32)
    return out


def _mod(ref, rep=1):
    if len(ref.shape) == 3:
        return ref[0]
    return ref[...] if rep == 1 else _repeat_rows(ref[...], rep)


def _log_sigmoid(z):
    return jnp.minimum(z, 0.0) - jnp.log1p(jnp.exp(-jnp.abs(z)))


def _adaln_kernel(c_ref, w_ref, b_ref, o_ref):
    @pl.when(pl.program_id(0) == 0)
    def _():
        o_ref[...] = jnp.broadcast_to(b_ref[...], o_ref.shape)

    c = c_ref[...]
    s = (c * jax.nn.sigmoid(c)).astype(BF16)
    o_ref[...] += jnp.dot(s, w_ref[...].astype(BF16), preferred_element_type=F32)


def _adaln(c, w_ada, b_ada):
    m, d = c.shape
    n = w_ada.shape[1]
    tk = LANES
    return pl.pallas_call(
        _adaln_kernel,
        grid=(d // tk,),
        in_specs=[pl.BlockSpec((m, tk), lambda k: (0, k)),
                  pl.BlockSpec((tk, n), lambda k: (k, 0)),
                  pl.BlockSpec((1, n), lambda k: (0, 0))],
        out_specs=pl.BlockSpec((m, n), lambda k: (0, 0)),
        out_shape=jax.ShapeDtypeStruct((m, n), F32),
        compiler_params=_params(1),
        name="adaln",
    )(c, w_ada, b_ada.reshape(1, n))


def _in_proj_kernel(*refs, tm, blocks_per_batch, per_row, rep, q_scale):
    if per_row:
        (x_ref, g_ref, sh_ref, sc_ref, w_ref, wf_ref, bf_ref, wc_ref, s0_ref, s1_ref,
         q_ref, kt_ref, vt_ref, kb_ref, vtb_ref, zc_ref, yb_ref, sga_ref, sgb_ref,
         lft_ref, h_scr) = refs
    else:
        (x_ref, g_ref, sh_ref, sc_ref, w_ref, wf_ref, bf_ref, wc_ref,
         q_ref, kt_ref, vt_ref, kb_ref, vtb_ref, zc_ref, yb_ref, sga_ref, sgb_ref,
         lft_ref, h_scr, halo_scr) = refs
    i = pl.program_id(0)
    j = pl.program_id(1)

    @pl.when(j == 0)
    def _():
        h = _rms(x_ref[...], g_ref[...]) * (1.0 + _mod(sc_ref, rep)) + _mod(sh_ref, rep)
        hb = h.astype(BF16)
        h_scr[...] = hb
        f = jnp.dot(hb, wf_ref[...], preferred_element_type=F32)
        lf = _log_sigmoid(f + bf_ref[...])
        lft_ref[0] = lf.T[0:N_HEADS, :]

    res = jnp.dot(h_scr[...], w_ref[...], preferred_element_type=F32)
    cw = D_ATTN // N_CHUNK
    gw = D_MODEL // N_CHUNK
    u = res[:, 0:cw]
    cg = res[:, cw:2 * cw]
    bg = res[:, 2 * cw:3 * cw]
    ga = res[:, 3 * cw:3 * cw + gw]
    gb = res[:, 3 * cw + gw:3 * cw + 2 * gw]
    o = 3 * cw + 2 * gw
    q = res[:, o:o + cw]
    k = res[:, o + cw:o + 2 * cw]
    v = res[:, o + 2 * cw:o + 3 * cw]

    q_ref[0] = (q * q_scale).astype(BF16)
    kt_ref[0] = k.T
    vt = v.T
    vt_ref[0] = vt
    vtb_ref[0] = vt.astype(BF16)
    kb_ref[0] = k.astype(BF16)
    sga_ref[...] = jax.nn.sigmoid(ga).astype(BF16)
    sgb_ref[...] = jax.nn.sigmoid(gb).astype(BF16)

    zc = cg * u
    zc_ref[...] = zc
    row = lax.broadcasted_iota(jnp.int32, zc.shape, 0)
    r1 = pltpu.roll(zc, 1, 0)
    r2 = pltpu.roll(zc, 2, 0)
    if per_row:
        t = row & 3
        s0 = s0_ref[...]
        s1 = s1_ref[...]
        p1 = jnp.where(t >= 1, r1, s1)
        p2 = jnp.where(t >= 2, r2, jnp.where(t == 1, s1, s0))
    else:
        first = (i % blocks_per_batch) == 0
        halo = halo_scr[j]
        h6 = jnp.where(first, 0.0, halo[6:7, :])
        h7 = jnp.where(first, 0.0, halo[7:8, :])
        p1 = jnp.where(row == 0, h7, r1)
        p2 = jnp.where(row == 0, h6, jnp.where(row == 1, h7, r2))
        halo_scr[j] = zc[tm - 8:tm, :]
    wc = wc_ref[...]
    y = p2 * wc[0:1, :]
    y = y + p1 * wc[1:2, :]
    y = y + zc * wc[2:3, :]
    yb_ref[...] = (bg * y).astype(BF16)


def _in_proj(x, g_pre, shift, scale, w_main, w_f, b_f, w_conv, state, *, tm, rows_per_batch):
    n = x.shape[0]
    per_row = state is not None
    nblk = n // tm
    cw = D_ATTN // N_CHUNK
    gw = D_MODEL // N_CHUNK
    if per_row:
        bpb = 1
        mod_spec = lambda c: pl.BlockSpec((tm // rows_per_batch, D_MODEL), lambda i, j: (i, c))
    else:
        bpb = rows_per_batch // tm
        mod_spec = lambda c: pl.BlockSpec((1, 1, D_MODEL), lambda i, j: (i // bpb, 0, c))
    in_specs = [
        pl.BlockSpec((tm, D_MODEL), lambda i, j: (i, 0)),
        pl.BlockSpec((1, D_MODEL), lambda i, j: (0, 0)),
        mod_spec(0), mod_spec(1),
        pl.BlockSpec((D_MODEL, CHUNK_W), lambda i, j: (0, j)),
        pl.BlockSpec((D_MODEL, LANES), lambda i, j: (0, 0)),
        pl.BlockSpec((1, LANES), lambda i, j: (0, 0)),
        pl.BlockSpec((CONV_W, cw), lambda i, j: (0, j)),
    ]
    args = [x, g_pre, shift, scale, w_main, w_f, b_f, w_conv]
    scratch = [pltpu.VMEM((tm, D_MODEL), BF16)]
    if per_row:
        in_specs += [pl.BlockSpec((tm, cw), lambda i, j: (i, j))] * 2
        args += [state[0], state[1]]
    else:
        scratch.append(pltpu.VMEM((N_CHUNK, 8, cw), F32))
    hp_spec = pl.BlockSpec((1, tm, LANES), lambda i, j: (j, i, 0))
    col_spec = lambda w: pl.BlockSpec((tm, w), lambda i, j: (i, j))
    nbat = nblk // bpb
    t_spec = lambda rows: pl.BlockSpec(
        (1, rows, tm), lambda i, j: (i // bpb, j if rows == LANES else 0, i % bpb))
    t_shape = lambda rows, dt=F32: jax.ShapeDtypeStruct((nbat, rows, bpb * tm), dt)
    out_specs = [hp_spec, t_spec(LANES), t_spec(LANES), hp_spec, t_spec(LANES),
                 col_spec(cw), col_spec(cw), col_spec(gw), col_spec(gw),
                 t_spec(N_HEADS)]
    hp_shape = jax.ShapeDtypeStruct((N_HP, n, LANES), BF16)
    out_shape = [hp_shape, t_shape(D_ATTN), t_shape(D_ATTN), hp_shape, t_shape(D_ATTN, BF16),
                 jax.ShapeDtypeStruct((n, D_CONV), F32), jax.ShapeDtypeStruct((n, D_CONV), BF16),
                 jax.ShapeDtypeStruct((n, D_MODEL), BF16), jax.ShapeDtypeStruct((n, D_MODEL), BF16),
                 t_shape(N_HEADS)]
    return pl.pallas_call(
        functools.partial(_in_proj_kernel, tm=tm, blocks_per_batch=bpb, per_row=per_row,
                          rep=rows_per_batch if per_row else 1,
                          q_scale=ATTN_SCALE if per_row else ATTN_SCALE * LOG2E),
        grid=(nblk, N_CHUNK),
        in_specs=in_specs, out_specs=out_specs, out_shape=out_shape,
        scratch_shapes=scratch,
        compiler_params=_params(2),
        name="in_proj_sample" if per_row else "in_proj_prompt",
    )(*args)


def _lane_cumsum(x):
    n = x.shape[-1]
    lane = lax.broadcasted_iota(jnp.int32, x.shape, x.ndim - 1)
    s = 1
    while s < n:
        x = x + jnp.where(lane >= s, pltpu.roll(x, s, x.ndim - 1), 0.0)
        s *= 2
    return x


def _cumsum_kernel(lft_ref, cumt_ref):
    cumt_ref[0] = _lane_cumsum(lft_ref[0]) * LOG2E


def _cumsum(lft, n_batch, t):
    spec = pl.BlockSpec((1, N_HEADS, t), lambda b: (b, 0, 0))
    return pl.pallas_call(
        _cumsum_kernel,
        grid=(n_batch,),
        in_specs=[spec], out_specs=spec,
        out_shape=jax.ShapeDtypeStruct((n_batch, N_HEADS, t), F32),
        compiler_params=_params(1),
        name="cumsum",
    )(lft)


ACC_ROWS = HEAD_DIM + 16


def _prompt_attn_kernel(qi_ref, kj_ref, q_ref, k_ref, vt_ref, cq_ref, ck_ref, o_ref,
                        acc_scr, m_scr, lg_scr, st_scr, mask_scr, *, tq, tk):
    t = pl.program_id(1)
    qi = qi_ref[t]
    kj = kj_ref[t]
    assert tq == tk

    @pl.when(kj == 0)
    def _():
        acc_scr[...] = jnp.zeros_like(acc_scr)
        m_scr[...] = jnp.full_like(m_scr, -jnp.inf)

    lane = lax.broadcasted_iota(jnp.int32, (1, LANES), 1)
    lo_half = lane < HEAD_DIM
    ones_rows = jnp.ones((ACC_ROWS - HEAD_DIM, tk), BF16)
    row8 = lax.broadcasted_iota(jnp.int32, (8, tq), 0)

    def logits(hp, slot, masked):
        q2 = q_ref[hp]
        k2 = k_ref[hp]
        m_pair = m_scr[hp]
        m_out = []
        stats = []
        for e in range(2):
            head = 2 * hp + e
            sel = lo_half if e == 0 else jnp.logical_not(lo_half)
            km = jnp.where(sel, k2, jnp.zeros_like(k2))
            st = lax.dot_general(km, q2, (((1,), (1,)), ((), ())),
                                 preferred_element_type=F32)
            ck_col = jnp.broadcast_to(ck_ref[0, head:head + 1, :], (LANES, tk)).T
            lg = st - pltpu.repeat(ck_col, tq // LANES, axis=1)
            if masked:
                lg = lg + mask_scr[...]
            cq = cq_ref[0, head:head + 1, :]
            m_prev = m_pair[e:e + 1, :]
            m_new = jnp.maximum(m_prev, jnp.max(lg, axis=0, keepdims=True) + cq)
            lg_scr[slot, e] = lg
            stats += [cq - m_new, jnp.exp2(m_prev - m_new)]
            m_out.append(m_new)
        st_scr[slot] = jnp.where(row8 == 0, stats[0], jnp.where(
            row8 == 1, stats[1], jnp.where(row8 == 2, stats[2], stats[3])))
        m_scr[hp] = jnp.where(row8 == 0, m_out[0], m_out[1])

    def update(hp, slot, last):
        vt_pair = vt_ref[0, hp * LANES:(hp + 1) * LANES, :]
        acc_pair = acc_scr[hp]
        stat = st_scr[slot]
        acc_out = []
        for e in range(2):
            p = jnp.exp2(lg_scr[slot, e] + stat[2 * e:2 * e + 1, :]).astype(BF16)
            vaug = jnp.concatenate(
                [vt_pair[e * HEAD_DIM:(e + 1) * HEAD_DIM, :], ones_rows], axis=0)
            acc_out.append(acc_pair[e * ACC_ROWS:(e + 1) * ACC_ROWS, :]
                           * stat[2 * e + 1:2 * e + 2, :]
                           + jnp.dot(vaug, p, preferred_element_type=F32))
        if last:
            halves = [a[0:HEAD_DIM, :] * (1.0 / a[HEAD_DIM:HEAD_DIM + 1, :]) for a in acc_out]
            o_ref[hp] = jnp.concatenate(halves, axis=0).T.astype(BF16)
        else:
            acc_scr[hp] = jnp.concatenate(acc_out, axis=0)

    def all_pairs(diagonal):
        logits(0, 0, diagonal)
        for hp in range(N_HP):
            if hp + 1 < N_HP:
                logits(hp + 1, (hp + 1) % 2, diagonal)
            update(hp, hp % 2, diagonal)

    @pl.when(kj < qi)
    def _():
        all_pairs(False)

    @pl.when(kj == qi)
    def _():
        key = lax.broadcasted_iota(jnp.int32, (tk, tq), 0)
        qry = lax.broadcasted_iota(jnp.int32, (tk, tq), 1)
        mask_scr[...] = jnp.where(key <= qry, 0.0, -jnp.inf)
        all_pairs(True)


def _prompt_attn(q_hp, k_hp, vt_b, cumt, n_batch, t, tq, tk):
    nq = t // tq
    nk = t // tk
    pairs = [(a, b) for a in range(nq) for b in range((a + 1) * tq // tk)]
    qi_tab = jnp.asarray([p[0] for p in pairs], jnp.int32)
    kj_tab = jnp.asarray([p[1] for p in pairs], jnp.int32)
    qblk = (N_HP, tq, LANES)
    grid_spec = pltpu.PrefetchScalarGridSpec(
        num_scalar_prefetch=2,
        grid=(n_batch, len(pairs)),
        in_specs=[
            pl.BlockSpec(qblk, lambda b, s, qi, kj: (0, b * nq + qi[s], 0)),
            pl.BlockSpec((N_HP, tk, LANES), lambda b, s, qi, kj: (0, b * nk + kj[s], 0)),
            pl.BlockSpec((1, D_ATTN, tk), lambda b, s, qi, kj: (b, 0, kj[s])),
            pl.BlockSpec((1, N_HEADS, tq), lambda b, s, qi, kj: (b, 0, qi[s])),
            pl.BlockSpec((1, N_HEADS, tk), lambda b, s, qi, kj: (b, 0, kj[s])),
        ],
        out_specs=pl.BlockSpec(qblk, lambda b, s, qi, kj: (0, b * nq + qi[s], 0)),
        scratch_shapes=[pltpu.VMEM((N_HP, 2 * ACC_ROWS, tq), F32),
                        pltpu.VMEM((N_HP, 8, tq), F32),
                        pltpu.VMEM((2, 2, tk, tq), F32),
                        pltpu.VMEM((2, 8, tq), F32),
                        pltpu.VMEM((tk, tq), F32)],
    )
    return pl.pallas_call(
        functools.partial(_prompt_attn_kernel, tq=tq, tk=tk),
        grid_spec=grid_spec,
        out_shape=jax.ShapeDtypeStruct((N_HP, n_batch * t, LANES), BF16),
        compiler_params=_params(2),
        name="prompt_attn",
    )(qi_tab, kj_tab, q_hp, k_hp, vt_b, cumt, cumt)


def _sample_attn_kernel(*refs, n_pages, n_new):
    pt_ref = refs[0]
    q_ref, kn_ref, vn_ref, lfn_ref = refs[1:5]
    k_refs = refs[5:5 + n_pages]
    v_refs = refs[5 + n_pages:5 + 2 * n_pages]
    lf_refs = refs[5 + 2 * n_pages:5 + 3 * n_pages]
    o_ref = refs[5 + 3 * n_pages]
    del pt_ref
    nr = n_new * N_HEADS
    nt = ((1,), (1,)), ((), ())
    page_t = lambda ref: ref[0].reshape(D_ATTN, PAGE_SIZE).astype(BF16)

    q = q_ref[0]
    hrow = lax.broadcasted_iota(jnp.int32, (N_HEADS, D_ATTN), 0)
    hlane = lax.broadcasted_iota(jnp.int32, (N_HEADS, D_ATTN), 1) // HEAD_DIM
    hmask = hrow == hlane
    qbd = jnp.concatenate(
        [jnp.where(hmask, jnp.broadcast_to(q[qq:qq + 1, :], (N_HEADS, D_ATTN)), 0.0)
         for qq in range(n_new)], axis=0).astype(BF16)

    s_parts = []
    lf_parts = []
    for c in range(n_pages):
        s_parts.append(jnp.dot(qbd, page_t(k_refs[c]), preferred_element_type=F32))
        lf_parts.append(lf_refs[c][0])
    s_past = jnp.concatenate(s_parts, axis=1)
    ck_past = _lane_cumsum(jnp.concatenate(lf_parts, axis=1))
    n_past = ck_past.shape[1]
    c_total = ck_past[:, n_past - 1:n_past]

    kn = jnp.concatenate(
        [kn_ref[0], jnp.zeros((PAGE_SIZE - 8, D_ATTN), F32)], axis=0).astype(BF16)
    vn = jnp.concatenate(
        [vn_ref[0], jnp.zeros((PAGE_SIZE - 8, D_ATTN), F32)], axis=0).astype(BF16)
    s_new = lax.dot_general(qbd, kn, nt, preferred_element_type=F32)
    ck_new = c_total + _lane_cumsum(lfn_ref[0])

    tile = lambda a: jnp.concatenate([a] * n_new, axis=0)
    ck_past_r = tile(ck_past)
    ck_new_r = tile(ck_new)
    qq_row = lax.broadcasted_iota(jnp.int32, (nr, PAGE_SIZE), 0) // N_HEADS
    lane = lax.broadcasted_iota(jnp.int32, (nr, PAGE_SIZE), 1)
    cq = jnp.sum(jnp.where(lane == qq_row, ck_new_r, 0.0), axis=1, keepdims=True)

    lg_past = s_past + (cq - ck_past_r)
    lg_new = jnp.where(lane <= qq_row, s_new + (cq - ck_new_r), -jnp.inf)
    m = jnp.maximum(jnp.max(lg_past, axis=1, keepdims=True),
                    jnp.max(lg_new, axis=1, keepdims=True))
    p_past = jnp.exp(lg_past - m)
    p_new = jnp.exp(lg_new - m)
    denom = jnp.sum(p_past, axis=1, keepdims=True) + jnp.sum(p_new, axis=1, keepdims=True)
    acc = jnp.dot(p_new.astype(BF16), vn, preferred_element_type=F32)
    pb = p_past.astype(BF16)
    for c in range(n_pages):
        acc = acc + lax.dot_general(pb[:, c * PAGE_SIZE:(c + 1) * PAGE_SIZE], page_t(v_refs[c]),
                                    nt, preferred_element_type=F32)
    acc = acc / denom
    orow = lax.broadcasted_iota(jnp.int32, (8, D_ATTN), 0)
    out = jnp.zeros((8, D_ATTN), F32)
    for qq in range(n_new):
        blk = jnp.where(hmask, acc[qq * N_HEADS:(qq + 1) * N_HEADS, :], 0.0)
        out = jnp.where(orow == qq, jnp.sum(blk, axis=0, keepdims=True), out)
    o_ref[0] = out


def _sample_attn(page_table, q8, kn8, vn8, lfn_t, cache_kt, cache_vt, cache_lft, n_new):
    n_batch, n_pages = page_table.shape
    row_spec = lambda w: pl.BlockSpec((1, 8, w), lambda b, pt: (b, 0, 0))
    kv_page = lambda c: pl.BlockSpec((1, N_HEADS, HEAD_DIM, PAGE_SIZE),
                                     lambda b, pt: (pt[b * n_pages + c], 0, 0, 0))
    lf_page = lambda c: pl.BlockSpec((1, N_HEADS, PAGE_SIZE),
                                     lambda b, pt: (pt[b * n_pages + c], 0, 0))
    in_specs = [row_spec(D_ATTN), row_spec(D_ATTN), row_spec(D_ATTN),
                pl.BlockSpec((1, N_HEADS, LANES), lambda b, pt: (b, 0, 0))]
    in_specs += [kv_page(c) for c in range(n_pages)]
    in_specs += [kv_page(c) for c in range(n_pages)]
    in_specs += [lf_page(c) for c in range(n_pages)]
    grid_spec = pltpu.PrefetchScalarGridSpec(
        num_scalar_prefetch=1,
        grid=(n_batch,),
        in_specs=in_specs,
        out_specs=row_spec(D_ATTN),
    )
    return pl.pallas_call(
        functools.partial(_sample_attn_kernel, n_pages=n_pages, n_new=n_new),
        grid_spec=grid_spec,
        out_shape=jax.ShapeDtypeStruct((n_batch, 8, D_ATTN), F32),
        compiler_params=_params(1),
        name="sample_attn",
    )(page_table.reshape(-1), q8, kn8, vn8, lfn_t,
      *([cache_kt] * n_pages), *([cache_vt] * n_pages), *([cache_lft] * n_pages))


def _merge_kernel(o_ref, yb_ref, sga_ref, sgb_ref, x_ref, gt_ref, sh_ref, sc_ref,
                  gp1_ref, gp2_ref, woa_ref, wob_ref, wo_ref, x1_ref, h2_ref, *, rep):
    tm = x_ref.shape[0]
    gate, shift, scale = _mod(gt_ref, rep), _mod(sh_ref, rep), _mod(sc_ref, rep)
    n_split = 2 if tm % 512 == 0 else 1
    for r in range(n_split):
        rows = slice(r * tm // n_split, (r + 1) * tm // n_split)
        sub = lambda v: v if v.shape[0] == 1 else v[rows, :]
        o = jnp.concatenate([o_ref[hp, rows, :] for hp in range(N_HP)], axis=1)
        a = jnp.dot(o, woa_ref[...], preferred_element_type=F32)
        b = jnp.dot(yb_ref[rows, :], wob_ref[...], preferred_element_type=F32)
        mm = (sga_ref[rows, :].astype(F32) * a + sgb_ref[rows, :].astype(F32) * b).astype(BF16)
        m = jnp.dot(mm, wo_ref[...], preferred_element_type=F32)
        x1 = x_ref[rows, :] + sub(gate) * _rms(m, gp1_ref[...])
        x1_ref[rows, :] = x1
        h2 = _rms(x1, gp2_ref[...]) * (1.0 + sub(scale)) + sub(shift)
        h2_ref[rows, :] = h2.astype(BF16)


def _merge(o_hp, yb, sga, sgb, x, mods, g_post1, g_pre2, w_oa, w_ob, w_o, *, tm,
           rows_per_batch, per_row):
    n = x.shape[0]
    if per_row:
        mod_spec = lambda c: pl.BlockSpec((tm // rows_per_batch, D_MODEL), lambda i: (i, c))
    else:
        bpb = rows_per_batch // tm
        mod_spec = lambda c: pl.BlockSpec((1, 1, D_MODEL), lambda i: (i // bpb, 0, c))
    row_spec = lambda w: pl.BlockSpec((tm, w), lambda i: (i, 0))
    const = lambda shape: pl.BlockSpec(shape, lambda i: (0,) * len(shape),
                                       pipeline_mode=pl.Buffered(1))
    return pl.pallas_call(
        functools.partial(_merge_kernel, rep=rows_per_batch if per_row else 1),
        grid=(n // tm,),
        in_specs=[pl.BlockSpec((N_HP, tm, LANES), lambda i: (0, i, 0)),
                  row_spec(D_CONV), row_spec(D_MODEL), row_spec(D_MODEL), row_spec(D_MODEL),
                  mod_spec(2), mod_spec(3), mod_spec(4),
                  const((1, D_MODEL)), const((1, D_MODEL)),
                  const((D_ATTN, D_MODEL)), const((D_CONV, D_MODEL)),
                  const((D_MODEL, D_MODEL))],
        out_specs=[row_spec(D_MODEL), row_spec(D_MODEL)],
        out_shape=[jax.ShapeDtypeStruct((n, D_MODEL), F32),
                   jax.ShapeDtypeStruct((n, D_MODEL), BF16)],
        compiler_params=_params(1),
        name="merge_sample" if per_row else "merge_prompt",
    )(o_hp, yb, sga, sgb, x, mods, mods, mods, g_post1, g_pre2, w_oa, w_ob, w_o)


def _ffn_kernel(h_ref, wg_ref, wu_ref, wout_ref, x1_ref, gt_ref, gp_ref, y_ref, *rest,
                rep, emit_bf16):
    acc_scr = rest[-1]
    j = pl.program_id(1)

    @pl.when(j == 0)
    def _():
        acc_scr[...] = jnp.zeros_like(acc_scr)

    wg, wu, wout = wg_ref[...], wu_ref[...], wout_ref[...]
    if emit_bf16:
        wg, wu, wout = wg.astype(BF16), wu.astype(BF16), wout.astype(BF16)
        rest[0][...], rest[1][...], rest[2][...] = wg, wu, wout
    h = h_ref[...]
    g = jnp.dot(h, wg, preferred_element_type=F32)
    u = jnp.dot(h, wu, preferred_element_type=F32)
    act = (g * jax.nn.sigmoid(g) * u).astype(BF16)
    acc_scr[...] += jnp.dot(act, wout, preferred_element_type=F32)

    @pl.when(j == pl.num_programs(1) - 1)
    def _():
        y_ref[...] = x1_ref[...] + _mod(gt_ref, rep) * _rms(acc_scr[...], gp_ref[...])


def _ffn(h2, x1, mods, g_post2, w_gate, w_up, w_out, *, tm, tf, rows_per_batch, per_row,
         emit_bf16=False):
    n = x1.shape[0]
    nf = D_FF // tf
    if per_row:
        mod_spec = pl.BlockSpec((tm // rows_per_batch, D_MODEL), lambda i, j: (i, 5))
    else:
        bpb = rows_per_batch // tm
        mod_spec = pl.BlockSpec((1, 1, D_MODEL), lambda i, j: (i // bpb, 0, 5))
    row_spec = pl.BlockSpec((tm, D_MODEL), lambda i, j: (i, 0))
    col0 = [c // tf for _, c in (w_gate, w_up)]
    in_col_spec = lambda c0: pl.BlockSpec((D_MODEL, tf), lambda i, j: (0, j + c0))
    col_spec = pl.BlockSpec((D_MODEL, tf), lambda i, j: (0, j))
    out_w_spec = pl.BlockSpec((tf, D_MODEL), lambda i, j: (j, 0))
    out_specs = [row_spec]
    out_shape = [jax.ShapeDtypeStruct((n, D_MODEL), F32)]
    if emit_bf16:
        assert n == tm
        out_specs += [col_spec, col_spec, out_w_spec]
        out_shape += [jax.ShapeDtypeStruct((D_MODEL, D_FF), BF16)] * 2
        out_shape += [jax.ShapeDtypeStruct((D_FF, D_MODEL), BF16)]
    outs = pl.pallas_call(
        functools.partial(_ffn_kernel, rep=rows_per_batch if per_row else 1,
                          emit_bf16=emit_bf16),
        grid=(n // tm, nf),
        in_specs=[row_spec, in_col_spec(col0[0]), in_col_spec(col0[1]), out_w_spec,
                  row_spec, mod_spec,
                  pl.BlockSpec((1, D_MODEL), lambda i, j: (0, 0))],
        out_specs=out_specs, out_shape=out_shape,
        scratch_shapes=[pltpu.VMEM((tm, D_MODEL), F32)],
        compiler_params=_params(2),
        name="ffn_sample" if per_row else "ffn_prompt",
    )(h2, w_gate[0], w_up[0], w_out, x1, mods, g_post2)
    return outs if emit_bf16 else outs[0]


def _w_chunk_kernel(tab_ref, *refs):
    del tab_ref
    f_ref, o_ref, of_ref = refs[-3:]
    for pos, w_ref in enumerate(refs[:-3]):
        o_ref[:, pos * LANES:(pos + 1) * LANES] = w_ref[...].T.astype(BF16)

    @pl.when(pl.program_id(0) == 0)
    def _():
        ft = f_ref[...].T
        lane = lax.broadcasted_iota(jnp.int32, ft.shape, 1)
        of_ref[...] = jnp.where(lane < N_HEADS, ft, 0.0).astype(BF16)


def _chunked_in_weights(w_in):
    d = w_in.shape[0]
    offs = np.cumsum([0, D_ATTN, D_ATTN, D_ATTN, N_HEADS, D_CONV, D_CONV, D_CONV,
                      D_MODEL, D_MODEL])
    q, k, v, f, u, bg, cg, ga, gb = [int(o) for o in offs[:9]]
    cw = D_ATTN // N_CHUNK
    gw = D_MODEL // N_CHUNK
    starts = []
    for j in range(N_CHUNK):
        starts += [u + cw * j, cg + cw * j, bg + cw * j]
        starts += [ga + gw * j + s for s in range(0, gw, LANES)]
        starts += [gb + gw * j + s for s in range(0, gw, LANES)]
        starts += [q + cw * j, k + cw * j, v + cw * j]
    n_slab = CHUNK_W // LANES
    slab = lambda pos: pl.BlockSpec((pl.Element(LANES), pl.Element(d)),
                                    lambda j, tab: (pl.multiple_of(tab[j * n_slab + pos], 8), 0))
    f_slab = pl.BlockSpec((pl.Element(LANES), pl.Element(d)), lambda j, tab: (f, 0))
    return pl.pallas_call(
        _w_chunk_kernel,
        grid_spec=pltpu.PrefetchScalarGridSpec(
            num_scalar_prefetch=1, grid=(N_CHUNK,),
            in_specs=[slab(pos) for pos in range(n_slab)] + [f_slab],
            out_specs=[pl.BlockSpec((d, CHUNK_W), lambda j, tab: (0, j)),
                       pl.BlockSpec((d, LANES), lambda j, tab: (0, 0))]),
        out_shape=[jax.ShapeDtypeStruct((d, N_CHUNK * CHUNK_W), BF16),
                   jax.ShapeDtypeStruct((d, LANES), BF16)],
        compiler_params=_params(1),
        name="w_chunk",
    )(jnp.asarray(starts, jnp.int32), *([w_in.T] * (n_slab + 1)))


def kernel(x_prompt, x_sample, cache_k, cache_v, cache_logf, state_conv, page_table,
           c_prompt, c_sample, w_ada, b_ada, g_pre1, w_in, b_f, w_conv, w_oa, w_ob, w_o,
           g_post1, g_pre2, w_ffn_in, w_ffn_out, g_post2):
    depth = w_in.shape[0]
    assert depth == 1
    nb, t, d = x_prompt.shape
    db, ts, _ = x_sample.shape
    l = 0

    w_main, w_f = _chunked_in_weights(w_in[l])
    b_f_pad = jnp.pad(b_f[l], (0, LANES - N_HEADS)).reshape(1, LANES)
    w_oa_b = w_oa[l].astype(BF16)
    w_ob_b = w_ob[l].astype(BF16)
    w_o_b = w_o[l].astype(BF16)
    row2 = lambda a: a.reshape(1, -1)

    mods = _adaln(jnp.concatenate([c_prompt, c_sample], axis=0), w_ada[l], b_ada[l])
    mods_p = mods[:nb].reshape(nb, 1, 6 * d)
    mods_s = mods[nb:]

    xp = x_prompt.reshape(nb * t, d)
    (q_hp, kt, vt, k_hp, vt_b, zc, yb, sga, sgb, lft) = _in_proj(
        xp, row2(g_pre1[l]), mods_p, mods_p, w_main, w_f, b_f_pad, w_conv[l], None,
        tm=1024, rows_per_batch=t)
    cumt = _cumsum(lft, nb, t)
    o_hp = _prompt_attn(q_hp, k_hp, vt_b, cumt, nb, t, 512, 512)
    x1, h2 = _merge(o_hp, yb, sga, sgb, xp, mods_p, row2(g_post1[l]), row2(g_pre2[l]),
                    w_oa_b, w_ob_b, w_o_b, tm=512, rows_per_batch=t, per_row=False)

    ns = db * ts
    xs = x_sample.reshape(ns, d)
    st = state_conv[l].astype(F32)
    state = (jnp.repeat(st[:, 0, :], ts, axis=0), jnp.repeat(st[:, 1, :], ts, axis=0))
    (qs_hp, kst, vst, _, _, zcs, ybs, sgas, sgbs, lfst) = _in_proj(
        xs, row2(g_pre1[l]), mods_s, mods_s, w_main, w_f, b_f_pad, w_conv[l], state,
        tm=ns, rows_per_batch=ts)
    ks_f = kst[0].T
    vs_f = vst[0].T
    lfs = lfst[0].reshape(N_HEADS, db, ts)
    pad8 = lambda a: jnp.pad(a.reshape(db, ts, -1), ((0, 0), (0, 8 - ts), (0, 0)))
    qs = qs_hp.transpose(1, 0, 2).reshape(ns, D_ATTN).astype(F32)
    lfn_t = jnp.pad(lfs.transpose(1, 0, 2), ((0, 0), (0, 0), (0, LANES - ts)))
    os8 = _sample_attn(page_table, pad8(qs), pad8(ks_f), pad8(vs_f), lfn_t,
                       cache_k[l].transpose(0, 2, 3, 1), cache_v[l].transpose(0, 2, 3, 1),
                       cache_logf[l].transpose(0, 2, 1), ts)
    os_hp = (os8[:, :ts, :].reshape(ns, N_HP, LANES).transpose(1, 0, 2).astype(BF16))
    x1s, h2s = _merge(os_hp, ybs, sgas, sgbs, xs, mods_s, row2(g_post1[l]), row2(g_pre2[l]),
                      w_oa_b, w_ob_b, w_o_b, tm=ns, rows_per_batch=ts, per_row=True)
    ys, w_gate_b, w_up_b, w_out_b = _ffn(
        h2s, x1s, mods_s, row2(g_post2[l]), (w_ffn_in[l], 0), (w_ffn_in[l], D_FF),
        w_ffn_out[l], tm=ns, tf=256, rows_per_batch=ts, per_row=True, emit_bf16=True)
    yp = _ffn(h2, x1, mods_p, row2(g_post2[l]), (w_gate_b, 0), (w_up_b, 0), w_out_b,
              tm=512, tf=512, rows_per_batch=t, per_row=False)

    heads = lambda a, b_, t_: a.reshape(1, b_, t_, N_HEADS, HEAD_DIM)
    heads_t = lambda a: a.reshape(nb, N_HEADS, HEAD_DIM, t).transpose(0, 3, 1, 2)[None]
    return (yp.reshape(nb, t, d), ys.reshape(db, ts, d),
            heads_t(kt), heads_t(vt), lft.transpose(0, 2, 1)[None],
            zc.reshape(nb, t, D_CONV)[:, t - (CONV_W - 1):, :][None],
            heads(ks_f, db, ts), heads(vs_f, db, ts), lfs.transpose(1, 2, 0)[None],
            zcs.reshape(db, ts, D_CONV)[:, ts - (CONV_W - 1):, :][None])
```

```python
import functools

import jax
import jax.numpy as jnp
import numpy as np
from jax import lax
from jax.experimental import pallas as pl
from jax.experimental.pallas import tpu as pltpu

F32 = jnp.float32
BF16 = jnp.bfloat16

D_MODEL = 2048
N_HEADS = 16
HEAD_DIM = 64
D_ATTN = N_HEADS * HEAD_DIM
D_CONV = 1024
CONV_W = 3
D_FF = 5632
RMS_EPS = 1e-6
ATTN_SCALE = HEAD_DIM ** -0.5
LOG2E = 1.4426950408889634
PAGE_SIZE = 128

LANES = 128
N_HP = D_ATTN // LANES
N_CHUNK = 8
CHUNK_W = 6 * (D_ATTN // N_CHUNK) + 2 * (D_MODEL // N_CHUNK)
VMEM_LIMIT = 56 * 1024 * 1024

TM_IN_PROJ = 1024
TM_MERGE = 512
TM_FFN = 512
TF_FFN = 512
TF_FFN_CAST = 256
T_ATTN = 512


def _params(n_axes):
    return pltpu.CompilerParams(
        dimension_semantics=("arbitrary",) * n_axes, vmem_limit_bytes=VMEM_LIMIT)


def _rms(x, g):
    return x * lax.rsqrt(jnp.mean(x * x, axis=-1, keepdims=True) + RMS_EPS) * g


def _repeat_rows(m, rep):
    n = m.shape[0]
    assert rep & (rep - 1) == 0
    src = lax.broadcasted_iota(jnp.int32, (n * rep, n), 0) >> (rep.bit_length() - 1)
    col = lax.broadcasted_iota(jnp.int32, (n * rep, n), 1)
    onehot = jnp.where(src == col, 1.0, 0.0).astype(BF16)
    out = None
    for _ in range(3):
        piece = m.astype(BF16)
        part = jnp.dot(onehot, piece, preferred_element_type=F32)
        out = part if out is None else out + part
        m = m - piece.astype(F32)
    return out


def _mod(ref, rep=1):
    if len(ref.shape) == 3:
        return ref[0]
    return ref[...] if rep == 1 else _repeat_rows(ref[...], rep)


def _log_sigmoid(z):
    return jnp.minimum(z, 0.0) - jnp.log1p(jnp.exp(-jnp.abs(z)))


def _adaln_kernel(c_ref, w_ref, b_ref, o_ref):
    @pl.when(pl.program_id(0) == 0)
    def _():
        o_ref[...] = jnp.broadcast_to(b_ref[...], o_ref.shape)

    c = c_ref[...]
    s = (c * jax.nn.sigmoid(c)).astype(BF16)
    o_ref[...] += jnp.dot(s, w_ref[...].astype(BF16), preferred_element_type=F32)


def _adaln(c, w_ada, b_ada):
    m, d = c.shape
    n = w_ada.shape[1]
    tk = 2 * LANES
    return pl.pallas_call(
        _adaln_kernel,
        grid=(d // tk,),
        in_specs=[pl.BlockSpec((m, tk), lambda k: (0, k)),
                  pl.BlockSpec((tk, n), lambda k: (k, 0)),
                  pl.BlockSpec((1, n), lambda k: (0, 0))],
        out_specs=pl.BlockSpec((m, n), lambda k: (0, 0)),
        out_shape=jax.ShapeDtypeStruct((m, n), F32),
        compiler_params=_params(1),
        name="adaln",
    )(c, w_ada, b_ada.reshape(1, n))


def _in_proj_kernel(*refs, tm, blocks_per_batch, per_row, rep, q_scale):
    if per_row:
        (x_ref, g_ref, sh_ref, sc_ref, w_ref, wf_ref, bf_ref, wc_ref, s0_ref, s1_ref,
         q_ref, kt_ref, vt_ref, kb_ref, vtb_ref, zc_ref, yb_ref, sga_ref, sgb_ref,
         lft_ref, h_scr) = refs
    else:
        (x_ref, g_ref, sh_ref, sc_ref, w_ref, wf_ref, bf_ref, wc_ref,
         q_ref, kt_ref, vt_ref, kb_ref, vtb_ref, zc_ref, yb_ref, sga_ref, sgb_ref,
         lft_ref, h_scr, halo_scr) = refs
    i = pl.program_id(0)
    j = pl.program_id(1)

    @pl.when(j == 0)
    def _():
        h = _rms(x_ref[...], g_ref[...]) * (1.0 + _mod(sc_ref, rep)) + _mod(sh_ref, rep)
        hb = h.astype(BF16)
        h_scr[...] = hb
        f = jnp.dot(hb, wf_ref[...], preferred_element_type=F32)
        lf = _log_sigmoid(f + bf_ref[...])
        lft_ref[0] = lf.T[0:N_HEADS, :]

    res = jnp.dot(h_scr[...], w_ref[...], preferred_element_type=F32)
    cw = D_ATTN // N_CHUNK
    gw = D_MODEL // N_CHUNK
    u = res[:, 0:cw]
    cg = res[:, cw:2 * cw]
    bg = res[:, 2 * cw:3 * cw]
    ga = res[:, 3 * cw:3 * cw + gw]
    gb = res[:, 3 * cw + gw:3 * cw + 2 * gw]
    o = 3 * cw + 2 * gw
    q = res[:, o:o + cw]
    k = res[:, o + cw:o + 2 * cw]
    v = res[:, o + 2 * cw:o + 3 * cw]

    q_ref[0] = (q * q_scale).astype(BF16)
    kt_ref[0] = k.T
    vt = v.T
    vt_ref[0] = vt
    vtb_ref[0] = vt.astype(BF16)
    kb_ref[0] = k.astype(BF16)
    sga_ref[...] = jax.nn.sigmoid(ga).astype(BF16)
    sgb_ref[...] = jax.nn.sigmoid(gb).astype(BF16)

    zc = cg * u
    zc_ref[...] = zc
    row = lax.broadcasted_iota(jnp.int32, zc.shape, 0)
    r1 = pltpu.roll(zc, 1, 0)
    r2 = pltpu.roll(zc, 2, 0)
    if per_row:
        t = row & 3
        s0 = s0_ref[...]
        s1 = s1_ref[...]
        p1 = jnp.where(t >= 1, r1, s1)
        p2 = jnp.where(t >= 2, r2, jnp.where(t == 1, s1, s0))
    else:
        first = (i % blocks_per_batch) == 0
        halo = halo_scr[j]
        h6 = jnp.where(first, 0.0, halo[6:7, :])
        h7 = jnp.where(first, 0.0, halo[7:8, :])
        p1 = jnp.where(row == 0, h7, r1)
        p2 = jnp.where(row == 0, h6, jnp.where(row == 1, h7, r2))
        halo_scr[j] = zc[tm - 8:tm, :]
    wc = wc_ref[...]
    y = p2 * wc[0:1, :]
    y = y + p1 * wc[1:2, :]
    y = y + zc * wc[2:3, :]
    yb_ref[...] = (bg * y).astype(BF16)


def _in_proj(x, g_pre, shift, scale, w_main, w_f, b_f, w_conv, state, *, tm, rows_per_batch):
    n = x.shape[0]
    per_row = state is not None
    nblk = n // tm
    cw = D_ATTN // N_CHUNK
    gw = D_MODEL // N_CHUNK
    if per_row:
        bpb = 1
        mod_spec = lambda c: pl.BlockSpec((tm // rows_per_batch, D_MODEL), lambda i, j: (i, c))
    else:
        bpb = rows_per_batch // tm
        mod_spec = lambda c: pl.BlockSpec((1, 1, D_MODEL), lambda i, j: (i // bpb, 0, c))
    in_specs = [
        pl.BlockSpec((tm, D_MODEL), lambda i, j: (i, 0)),
        pl.BlockSpec((1, D_MODEL), lambda i, j: (0, 0)),
        mod_spec(0), mod_spec(1),
        pl.BlockSpec((D_MODEL, CHUNK_W), lambda i, j: (0, j)),
        pl.BlockSpec((D_MODEL, LANES), lambda i, j: (0, 0)),
        pl.BlockSpec((1, LANES), lambda i, j: (0, 0)),
        pl.BlockSpec((CONV_W, cw), lambda i, j: (0, j)),
    ]
    args = [x, g_pre, shift, scale, w_main, w_f, b_f, w_conv]
    scratch = [pltpu.VMEM((tm, D_MODEL), BF16)]
    if per_row:
        in_specs += [pl.BlockSpec((tm, cw), lambda i, j: (i, j))] * 2
        args += [state[0], state[1]]
    else:
        scratch.append(pltpu.VMEM((N_CHUNK, 8, cw), F32))
    hp_spec = pl.BlockSpec((1, tm, LANES), lambda i, j: (j, i, 0))
    col_spec = lambda w: pl.BlockSpec((tm, w), lambda i, j: (i, j))
    nbat = nblk // bpb
    t_spec = lambda rows: pl.BlockSpec(
        (1, rows, tm), lambda i, j: (i // bpb, j if rows == LANES else 0, i % bpb))
    t_shape = lambda rows, dt=F32: jax.ShapeDtypeStruct((nbat, rows, bpb * tm), dt)
    out_specs = [hp_spec, t_spec(LANES), t_spec(LANES), hp_spec, t_spec(LANES),
                 col_spec(cw), col_spec(cw), col_spec(gw), col_spec(gw),
                 t_spec(N_HEADS)]
    hp_shape = jax.ShapeDtypeStruct((N_HP, n, LANES), BF16)
    out_shape = [hp_shape, t_shape(D_ATTN), t_shape(D_ATTN), hp_shape, t_shape(D_ATTN, BF16),
                 jax.ShapeDtypeStruct((n, D_CONV), F32), jax.ShapeDtypeStruct((n, D_CONV), BF16),
                 jax.ShapeDtypeStruct((n, D_MODEL), BF16), jax.ShapeDtypeStruct((n, D_MODEL), BF16),
                 t_shape(N_HEADS)]
    return pl.pallas_call(
        functools.partial(_in_proj_kernel, tm=tm, blocks_per_batch=bpb, per_row=per_row,
                          rep=rows_per_batch if per_row else 1,
                          q_scale=ATTN_SCALE if per_row else ATTN_SCALE * LOG2E),
        grid=(nblk, N_CHUNK),
        in_specs=in_specs, out_specs=out_specs, out_shape=out_shape,
        scratch_shapes=scratch,
        compiler_params=_params(2),
        name="in_proj_sample" if per_row else "in_proj_prompt",
    )(*args)


def _lane_cumsum(x):
    n = x.shape[-1]
    lane = lax.broadcasted_iota(jnp.int32, x.shape, x.ndim - 1)
    s = 1
    while s < n:
        x = x + jnp.where(lane >= s, pltpu.roll(x, s, x.ndim - 1), 0.0)
        s *= 2
    return x


def _cumsum_kernel(lft_ref, cumt_ref):
    cumt_ref[0] = _lane_cumsum(lft_ref[0]) * LOG2E


def _cumsum(lft, n_batch, t):
    spec = pl.BlockSpec((1, N_HEADS, t), lambda b: (b, 0, 0))
    return pl.pallas_call(
        _cumsum_kernel,
        grid=(n_batch,),
        in_specs=[spec], out_specs=spec,
        out_shape=jax.ShapeDtypeStruct((n_batch, N_HEADS, t), F32),
        compiler_params=_params(1),
        name="cumsum",
    )(lft)


ACC_ROWS = HEAD_DIM + 16


def _prompt_attn_kernel(qi_ref, kj_ref, q_ref, k_ref, vt_ref, cq_ref, ck_ref, o_ref,
                        acc_scr, m_scr, lg_scr, st_scr, mask_scr, *, tq, tk):
    t = pl.program_id(1)
    qi = qi_ref[t]
    kj = kj_ref[t]
    assert tq == tk

    @pl.when(kj == 0)
    def _():
        acc_scr[...] = jnp.zeros_like(acc_scr)
        m_scr[...] = jnp.full_like(m_scr, -jnp.inf)

    lane = lax.broadcasted_iota(jnp.int32, (1, LANES), 1)
    lo_half = lane < HEAD_DIM
    ones_rows = jnp.ones((ACC_ROWS - HEAD_DIM, tk), BF16)
    row8 = lax.broadcasted_iota(jnp.int32, (8, tq), 0)

    def logits(hp, slot, masked):
        q2 = q_ref[hp]
        k2 = k_ref[hp]
        m_pair = m_scr[hp]
        m_out = []
        stats = []
        for e in range(2):
            head = 2 * hp + e
            sel = lo_half if e == 0 else jnp.logical_not(lo_half)
            km = jnp.where(sel, k2, jnp.zeros_like(k2))
            st = lax.dot_general(km, q2, (((1,), (1,)), ((), ())),
                                 preferred_element_type=F32)
            ck_col = jnp.broadcast_to(ck_ref[0, head:head + 1, :], (LANES, tk)).T
            lg = st - pltpu.repeat(ck_col, tq // LANES, axis=1)
            if masked:
                lg = lg + mask_scr[...]
            cq = cq_ref[0, head:head + 1, :]
            m_prev = m_pair[e:e + 1, :]
            m_new = jnp.maximum(m_prev, jnp.max(lg, axis=0, keepdims=True) + cq)
            lg_scr[slot, e] = lg
            stats += [cq - m_new, jnp.exp2(m_prev - m_new)]
            m_out.append(m_new)
        st_scr[slot] = jnp.where(row8 == 0, stats[0], jnp.where(
            row8 == 1, stats[1], jnp.where(row8 == 2, stats[2], stats[3])))
        m_scr[hp] = jnp.where(row8 == 0, m_out[0], m_out[1])

    def update(hp, slot, last):
        vt_pair = vt_ref[0, hp * LANES:(hp + 1) * LANES, :]
        acc_pair = acc_scr[hp]
        stat = st_scr[slot]
        acc_out = []
        for e in range(2):
            p = jnp.exp2(lg_scr[slot, e] + stat[2 * e:2 * e + 1, :]).astype(BF16)
            vaug = jnp.concatenate(
                [vt_pair[e * HEAD_DIM:(e + 1) * HEAD_DIM, :], ones_rows], axis=0)
            acc_out.append(acc_pair[e * ACC_ROWS:(e + 1) * ACC_ROWS, :]
                           * stat[2 * e + 1:2 * e + 2, :]
                           + jnp.dot(vaug, p, preferred_element_type=F32))
        if last:
            halves = [a[0:HEAD_DIM, :] * (1.0 / a[HEAD_DIM:HEAD_DIM + 1, :]) for a in acc_out]
            o_ref[hp] = jnp.concatenate(halves, axis=0).T.astype(BF16)
        else:
            acc_scr[hp] = jnp.concatenate(acc_out, axis=0)

    def all_pairs(diagonal):
        logits(0, 0, diagonal)
        for hp in range(N_HP):
            if hp + 1 < N_HP:
                logits(hp + 1, (hp + 1) % 2, diagonal)
            update(hp, hp % 2, diagonal)

    @pl.when(kj < qi)
    def _():
        all_pairs(False)

    @pl.when(kj == qi)
    def _():
        key = lax.broadcasted_iota(jnp.int32, (tk, tq), 0)
        qry = lax.broadcasted_iota(jnp.int32, (tk, tq), 1)
        mask_scr[...] = jnp.where(key <= qry, 0.0, -jnp.inf)
        all_pairs(True)


def _prompt_attn(q_hp, k_hp, vt_b, cumt, n_batch, t, tq, tk):
    nq = t // tq
    nk = t // tk
    pairs = [(a, b) for a in range(nq) for b in range((a + 1) * tq // tk)]
    qi_tab = jnp.asarray([p[0] for p in pairs], jnp.int32)
    kj_tab = jnp.asarray([p[1] for p in pairs], jnp.int32)
    qblk = (N_HP, tq, LANES)
    grid_spec = pltpu.PrefetchScalarGridSpec(
        num_scalar_prefetch=2,
        grid=(n_batch, len(pairs)),
        in_specs=[
            pl.BlockSpec(qblk, lambda b, s, qi, kj: (0, b * nq + qi[s], 0)),
            pl.BlockSpec((N_HP, tk, LANES), lambda b, s, qi, kj: (0, b * nk + kj[s], 0)),
            pl.BlockSpec((1, D_ATTN, tk), lambda b, s, qi, kj: (b, 0, kj[s])),
            pl.BlockSpec((1, N_HEADS, tq), lambda b, s, qi, kj: (b, 0, qi[s])),
            pl.BlockSpec((1, N_HEADS, tk), lambda b, s, qi, kj: (b, 0, kj[s])),
        ],
        out_specs=pl.BlockSpec(qblk, lambda b, s, qi, kj: (0, b * nq + qi[s], 0)),
        scratch_shapes=[pltpu.VMEM((N_HP, 2 * ACC_ROWS, tq), F32),
                        pltpu.VMEM((N_HP, 8, tq), F32),
                        pltpu.VMEM((2, 2, tk, tq), F32),
                        pltpu.VMEM((2, 8, tq), F32),
                        pltpu.VMEM((tk, tq), F32)],
    )
    return pl.pallas_call(
        functools.partial(_prompt_attn_kernel, tq=tq, tk=tk),
        grid_spec=grid_spec,
        out_shape=jax.ShapeDtypeStruct((N_HP, n_batch * t, LANES), BF16),
        compiler_params=_params(2),
        name="prompt_attn",
    )(qi_tab, kj_tab, q_hp, k_hp, vt_b, cumt, cumt)


def _sample_attn_kernel(*refs, n_pages, n_new):
    pt_ref = refs[0]
    q_ref, kn_ref, vn_ref, lfn_ref = refs[1:5]
    k_refs = refs[5:5 + n_pages]
    v_refs = refs[5 + n_pages:5 + 2 * n_pages]
    lf_refs = refs[5 + 2 * n_pages:5 + 3 * n_pages]
    o_ref = refs[5 + 3 * n_pages]
    del pt_ref
    nr = n_new * N_HEADS
    nt = ((1,), (1,)), ((), ())
    page_t = lambda ref: ref[0].reshape(D_ATTN, PAGE_SIZE).astype(BF16)

    q = q_ref[0]
    hrow = lax.broadcasted_iota(jnp.int32, (N_HEADS, D_ATTN), 0)
    hlane = lax.broadcasted_iota(jnp.int32, (N_HEADS, D_ATTN), 1) // HEAD_DIM
    hmask = hrow == hlane
    qbd = jnp.concatenate(
        [jnp.where(hmask, jnp.broadcast_to(q[qq:qq + 1, :], (N_HEADS, D_ATTN)), 0.0)
         for qq in range(n_new)], axis=0).astype(BF16)

    s_parts = []
    lf_parts = []
    for c in range(n_pages):
        s_parts.append(jnp.dot(qbd, page_t(k_refs[c]), preferred_element_type=F32))
        lf_parts.append(lf_refs[c][0])
    s_past = jnp.concatenate(s_parts, axis=1)
    ck_past = _lane_cumsum(jnp.concatenate(lf_parts, axis=1))
    n_past = ck_past.shape[1]
    c_total = ck_past[:, n_past - 1:n_past]

    kn = jnp.concatenate(
        [kn_ref[0], jnp.zeros((PAGE_SIZE - 8, D_ATTN), F32)], axis=0).astype(BF16)
    vn = jnp.concatenate(
        [vn_ref[0], jnp.zeros((PAGE_SIZE - 8, D_ATTN), F32)], axis=0).astype(BF16)
    s_new = lax.dot_general(qbd, kn, nt, preferred_element_type=F32)
    ck_new = c_total + _lane_cumsum(lfn_ref[0])

    tile = lambda a: jnp.concatenate([a] * n_new, axis=0)
    ck_past_r = tile(ck_past)
    ck_new_r = tile(ck_new)
    qq_row = lax.broadcasted_iota(jnp.int32, (nr, PAGE_SIZE), 0) // N_HEADS
    lane = lax.broadcasted_iota(jnp.int32, (nr, PAGE_SIZE), 1)
    cq = jnp.sum(jnp.where(lane == qq_row, ck_new_r, 0.0), axis=1, keepdims=True)

    lg_past = s_past + (cq - ck_past_r)
    lg_new = jnp.where(lane <= qq_row, s_new + (cq - ck_new_r), -jnp.inf)
    m = jnp.maximum(jnp.max(lg_past, axis=1, keepdims=True),
                    jnp.max(lg_new, axis=1, keepdims=True))
    p_past = jnp.exp(lg_past - m)
    p_new = jnp.exp(lg_new - m)
    denom = jnp.sum(p_past, axis=1, keepdims=True) + jnp.sum(p_new, axis=1, keepdims=True)
    acc = jnp.dot(p_new.astype(BF16), vn, preferred_element_type=F32)
    pb = p_past.astype(BF16)
    for c in range(n_pages):
        acc = acc + lax.dot_general(pb[:, c * PAGE_SIZE:(c + 1) * PAGE_SIZE], page_t(v_refs[c]),
                                    nt, preferred_element_type=F32)
    acc = acc / denom
    orow = lax.broadcasted_iota(jnp.int32, (8, D_ATTN), 0)
    out = jnp.zeros((8, D_ATTN), F32)
    for qq in range(n_new):
        blk = jnp.where(hmask, acc[qq * N_HEADS:(qq + 1) * N_HEADS, :], 0.0)
        out = jnp.where(orow == qq, jnp.sum(blk, axis=0, keepdims=True), out)
    o_ref[0] = out


def _sample_attn(page_table, q8, kn8, vn8, lfn_t, cache_kt, cache_vt, cache_lft, n_new):
    n_batch, n_pages = page_table.shape
    row_spec = lambda w: pl.BlockSpec((1, 8, w), lambda b, pt: (b, 0, 0))
    kv_page = lambda c: pl.BlockSpec((1, N_HEADS, HEAD_DIM, PAGE_SIZE),
                                     lambda b, pt: (pt[b * n_pages + c], 0, 0, 0))
    lf_page = lambda c: pl.BlockSpec((1, N_HEADS, PAGE_SIZE),
                                     lambda b, pt: (pt[b * n_pages + c], 0, 0))
    in_specs = [row_spec(D_ATTN), row_spec(D_ATTN), row_spec(D_ATTN),
                pl.BlockSpec((1, N_HEADS, LANES), lambda b, pt: (b, 0, 0))]
    in_specs += [kv_page(c) for c in range(n_pages)]
    in_specs += [kv_page(c) for c in range(n_pages)]
    in_specs += [lf_page(c) for c in range(n_pages)]
    grid_spec = pltpu.PrefetchScalarGridSpec(
        num_scalar_prefetch=1,
        grid=(n_batch,),
        in_specs=in_specs,
        out_specs=row_spec(D_ATTN),
    )
    return pl.pallas_call(
        functools.partial(_sample_attn_kernel, n_pages=n_pages, n_new=n_new),
        grid_spec=grid_spec,
        out_shape=jax.ShapeDtypeStruct((n_batch, 8, D_ATTN), F32),
        compiler_params=_params(1),
        name="sample_attn",
    )(page_table.reshape(-1), q8, kn8, vn8, lfn_t,
      *([cache_kt] * n_pages), *([cache_vt] * n_pages), *([cache_lft] * n_pages))


def _merge_kernel(o_ref, yb_ref, sga_ref, sgb_ref, x_ref, gt_ref, sh_ref, sc_ref,
                  gp1_ref, gp2_ref, woa_ref, wob_ref, wo_ref, x1_ref, h2_ref, *, rep):
    tm = x_ref.shape[0]
    gate, shift, scale = _mod(gt_ref, rep), _mod(sh_ref, rep), _mod(sc_ref, rep)
    n_split = 2 if tm % 512 == 0 else 1
    for r in range(n_split):
        rows = slice(r * tm // n_split, (r + 1) * tm // n_split)
        sub = lambda v: v if v.shape[0] == 1 else v[rows, :]
        o = jnp.concatenate([o_ref[hp, rows, :] for hp in range(N_HP)], axis=1)
        a = jnp.dot(o, woa_ref[...], preferred_element_type=F32)
        b = jnp.dot(yb_ref[rows, :], wob_ref[...], preferred_element_type=F32)
        mm = (sga_ref[rows, :].astype(F32) * a + sgb_ref[rows, :].astype(F32) * b).astype(BF16)
        m = jnp.dot(mm, wo_ref[...], preferred_element_type=F32)
        x1 = x_ref[rows, :] + sub(gate) * _rms(m, gp1_ref[...])
        x1_ref[rows, :] = x1
        h2 = _rms(x1, gp2_ref[...]) * (1.0 + sub(scale)) + sub(shift)
        h2_ref[rows, :] = h2.astype(BF16)


def _merge(o_hp, yb, sga, sgb, x, mods, g_post1, g_pre2, w_oa, w_ob, w_o, *, tm,
           rows_per_batch, per_row):
    n = x.shape[0]
    if per_row:
        mod_spec = lambda c: pl.BlockSpec((tm // rows_per_batch, D_MODEL), lambda i: (i, c))
    else:
        bpb = rows_per_batch // tm
        mod_spec = lambda c: pl.BlockSpec((1, 1, D_MODEL), lambda i: (i // bpb, 0, c))
    row_spec = lambda w: pl.BlockSpec((tm, w), lambda i: (i, 0))
    const = lambda shape: pl.BlockSpec(shape, lambda i: (0,) * len(shape),
                                       pipeline_mode=pl.Buffered(1))
    return pl.pallas_call(
        functools.partial(_merge_kernel, rep=rows_per_batch if per_row else 1),
        grid=(n // tm,),
        in_specs=[pl.BlockSpec((N_HP, tm, LANES), lambda i: (0, i, 0)),
                  row_spec(D_CONV), row_spec(D_MODEL), row_spec(D_MODEL), row_spec(D_MODEL),
                  mod_spec(2), mod_spec(3), mod_spec(4),
                  const((1, D_MODEL)), const((1, D_MODEL)),
                  const((D_ATTN, D_MODEL)), const((D_CONV, D_MODEL)),
                  const((D_MODEL, D_MODEL))],
        out_specs=[row_spec(D_MODEL), row_spec(D_MODEL)],
        out_shape=[jax.ShapeDtypeStruct((n, D_MODEL), F32),
                   jax.ShapeDtypeStruct((n, D_MODEL), BF16)],
        compiler_params=_params(1),
        name="merge_sample" if per_row else "merge_prompt",
    )(o_hp, yb, sga, sgb, x, mods, mods, mods, g_post1, g_pre2, w_oa, w_ob, w_o)


def _ffn_kernel(h_ref, wg_ref, wu_ref, wout_ref, x1_ref, gt_ref, gp_ref, y_ref, *rest,
                rep, emit_bf16):
    acc_scr = rest[-1]
    j = pl.program_id(1)

    @pl.when(j == 0)
    def _():
        acc_scr[...] = jnp.zeros_like(acc_scr)

    wg, wu, wout = wg_ref[...], wu_ref[...], wout_ref[...]
    if emit_bf16:
        wg, wu, wout = wg.astype(BF16), wu.astype(BF16), wout.astype(BF16)
        rest[0][...], rest[1][...], rest[2][...] = wg, wu, wout
    h = h_ref[...]
    g = jnp.dot(h, wg, preferred_element_type=F32)
    u = jnp.dot(h, wu, preferred_element_type=F32)
    act = (g * jax.nn.sigmoid(g) * u).astype(BF16)
    acc_scr[...] += jnp.dot(act, wout, preferred_element_type=F32)

    @pl.when(j == pl.num_programs(1) - 1)
    def _():
        y_ref[...] = x1_ref[...] + _mod(gt_ref, rep) * _rms(acc_scr[...], gp_ref[...])


def _ffn(h2, x1, mods, g_post2, w_gate, w_up, w_out, *, tm, tf, rows_per_batch, per_row,
         emit_bf16=False):
    n = x1.shape[0]
    nf = D_FF // tf
    if per_row:
        mod_spec = pl.BlockSpec((tm // rows_per_batch, D_MODEL), lambda i, j: (i, 5))
    else:
        bpb = rows_per_batch // tm
        mod_spec = pl.BlockSpec((1, 1, D_MODEL), lambda i, j: (i // bpb, 0, 5))
    row_spec = pl.BlockSpec((tm, D_MODEL), lambda i, j: (i, 0))
    col0 = [c // tf for _, c in (w_gate, w_up)]
    in_col_spec = lambda c0: pl.BlockSpec((D_MODEL, tf), lambda i, j: (0, j + c0))
    col_spec = pl.BlockSpec((D_MODEL, tf), lambda i, j: (0, j))
    out_w_spec = pl.BlockSpec((tf, D_MODEL), lambda i, j: (j, 0))
    out_specs = [row_spec]
    out_shape = [jax.ShapeDtypeStruct((n, D_MODEL), F32)]
    if emit_bf16:
        assert n == tm
        out_specs += [col_spec, col_spec, out_w_spec]
        out_shape += [jax.ShapeDtypeStruct((D_MODEL, D_FF), BF16)] * 2
        out_shape += [jax.ShapeDtypeStruct((D_FF, D_MODEL), BF16)]
    outs = pl.pallas_call(
        functools.partial(_ffn_kernel, rep=rows_per_batch if per_row else 1,
                          emit_bf16=emit_bf16),
        grid=(n // tm, nf),
        in_specs=[row_spec, in_col_spec(col0[0]), in_col_spec(col0[1]), out_w_spec,
                  row_spec, mod_spec,
                  pl.BlockSpec((1, D_MODEL), lambda i, j: (0, 0))],
        out_specs=out_specs, out_shape=out_shape,
        scratch_shapes=[pltpu.VMEM((tm, D_MODEL), F32)],
        compiler_params=_params(2),
        name="ffn_sample" if per_row else "ffn_prompt",
    )(h2, w_gate[0], w_up[0], w_out, x1, mods, g_post2)
    return outs if emit_bf16 else outs[0]


def _w_chunk_kernel(tab_ref, *refs):
    del tab_ref
    f_ref, o_ref, of_ref = refs[-3:]
    for pos, w_ref in enumerate(refs[:-3]):
        o_ref[:, pos * LANES:(pos + 1) * LANES] = w_ref[...].T.astype(BF16)

    @pl.when(pl.program_id(0) == 0)
    def _():
        ft = f_ref[...].T
        lane = lax.broadcasted_iota(jnp.int32, ft.shape, 1)
        of_ref[...] = jnp.where(lane < N_HEADS, ft, 0.0).astype(BF16)


def _chunked_in_weights(w_in):
    d = w_in.shape[0]
    offs = np.cumsum([0, D_ATTN, D_ATTN, D_ATTN, N_HEADS, D_CONV, D_CONV, D_CONV,
                      D_MODEL, D_MODEL])
    q, k, v, f, u, bg, cg, ga, gb = [int(o) for o in offs[:9]]
    cw = D_ATTN // N_CHUNK
    gw = D_MODEL // N_CHUNK
    starts = []
    for j in range(N_CHUNK):
        starts += [u + cw * j, cg + cw * j, bg + cw * j]
        starts += [ga + gw * j + s for s in range(0, gw, LANES)]
        starts += [gb + gw * j + s for s in range(0, gw, LANES)]
        starts += [q + cw * j, k + cw * j, v + cw * j]
    n_slab = CHUNK_W // LANES
    slab = lambda pos: pl.BlockSpec((pl.Element(LANES), pl.Element(d)),
                                    lambda j, tab: (pl.multiple_of(tab[j * n_slab + pos], 8), 0))
    f_slab = pl.BlockSpec((pl.Element(LANES), pl.Element(d)), lambda j, tab: (f, 0))
    return pl.pallas_call(
        _w_chunk_kernel,
        grid_spec=pltpu.PrefetchScalarGridSpec(
            num_scalar_prefetch=1, grid=(N_CHUNK,),
            in_specs=[slab(pos) for pos in range(n_slab)] + [f_slab],
            out_specs=[pl.BlockSpec((d, CHUNK_W), lambda j, tab: (0, j)),
                       pl.BlockSpec((d, LANES), lambda j, tab: (0, 0))]),
        out_shape=[jax.ShapeDtypeStruct((d, N_CHUNK * CHUNK_W), BF16),
                   jax.ShapeDtypeStruct((d, LANES), BF16)],
        compiler_params=_params(1),
        name="w_chunk",
    )(jnp.asarray(starts, jnp.int32), *([w_in.T] * (n_slab + 1)))


def kernel(x_prompt, x_sample, cache_k, cache_v, cache_logf, state_conv, page_table,
           c_prompt, c_sample, w_ada, b_ada, g_pre1, w_in, b_f, w_conv, w_oa, w_ob, w_o,
           g_post1, g_pre2, w_ffn_in, w_ffn_out, g_post2):
    depth = w_in.shape[0]
    assert depth == 1
    nb, t, d = x_prompt.shape
    db, ts, _ = x_sample.shape
    l = 0

    w_main, w_f = _chunked_in_weights(w_in[l])
    b_f_pad = jnp.pad(b_f[l], (0, LANES - N_HEADS)).reshape(1, LANES)
    w_oa_b = w_oa[l].astype(BF16)
    w_ob_b = w_ob[l].astype(BF16)
    w_o_b = w_o[l].astype(BF16)
    row2 = lambda a: a.reshape(1, -1)

    mods = _adaln(jnp.concatenate([c_prompt, c_sample], axis=0), w_ada[l], b_ada[l])
    mods_p = mods[:nb].reshape(nb, 1, 6 * d)
    mods_s = mods[nb:]

    xp = x_prompt.reshape(nb * t, d)
    (q_hp, kt, vt, k_hp, vt_b, zc, yb, sga, sgb, lft) = _in_proj(
        xp, row2(g_pre1[l]), mods_p, mods_p, w_main, w_f, b_f_pad, w_conv[l], None,
        tm=TM_IN_PROJ, rows_per_batch=t)
    cumt = _cumsum(lft, nb, t)
    o_hp = _prompt_attn(q_hp, k_hp, vt_b, cumt, nb, t, T_ATTN, T_ATTN)
    x1, h2 = _merge(o_hp, yb, sga, sgb, xp, mods_p, row2(g_post1[l]), row2(g_pre2[l]),
                    w_oa_b, w_ob_b, w_o_b, tm=TM_MERGE, rows_per_batch=t, per_row=False)

    ns = db * ts
    xs = x_sample.reshape(ns, d)
    st = state_conv[l].astype(F32)
    state = (jnp.repeat(st[:, 0, :], ts, axis=0), jnp.repeat(st[:, 1, :], ts, axis=0))
    (qs_hp, kst, vst, _, _, zcs, ybs, sgas, sgbs, lfst) = _in_proj(
        xs, row2(g_pre1[l]), mods_s, mods_s, w_main, w_f, b_f_pad, w_conv[l], state,
        tm=ns, rows_per_batch=ts)
    ks_f = kst[0].T
    vs_f = vst[0].T
    lfs = lfst[0].reshape(N_HEADS, db, ts)
    pad8 = lambda a: jnp.pad(a.reshape(db, ts, -1), ((0, 0), (0, 8 - ts), (0, 0)))
    qs = qs_hp.transpose(1, 0, 2).reshape(ns, D_ATTN).astype(F32)
    lfn_t = jnp.pad(lfs.transpose(1, 0, 2), ((0, 0), (0, 0), (0, LANES - ts)))
    os8 = _sample_attn(page_table, pad8(qs), pad8(ks_f), pad8(vs_f), lfn_t,
                       cache_k[l].transpose(0, 2, 3, 1), cache_v[l].transpose(0, 2, 3, 1),
                       cache_logf[l].transpose(0, 2, 1), ts)
    os_hp = (os8[:, :ts, :].reshape(ns, N_HP, LANES).transpose(1, 0, 2).astype(BF16))
    x1s, h2s = _merge(os_hp, ybs, sgas, sgbs, xs, mods_s, row2(g_post1[l]), row2(g_pre2[l]),
                      w_oa_b, w_ob_b, w_o_b, tm=ns, rows_per_batch=ts, per_row=True)
    ys, w_gate_b, w_up_b, w_out_b = _ffn(
        h2s, x1s, mods_s, row2(g_post2[l]), (w_ffn_in[l], 0), (w_ffn_in[l], D_FF),
        w_ffn_out[l], tm=ns, tf=TF_FFN_CAST, rows_per_batch=ts, per_row=True, emit_bf16=True)
    yp = _ffn(h2, x1, mods_p, row2(g_post2[l]), (w_gate_b, 0), (w_up_b, 0), w_out_b,
              tm=TM_FFN, tf=TF_FFN, rows_per_batch=t, per_row=False)

    heads = lambda a, b_, t_: a.reshape(1, b_, t_, N_HEADS, HEAD_DIM)
    heads_t = lambda a: a.reshape(nb, N_HEADS, HEAD_DIM, t).transpose(0, 3, 1, 2)[None]
    return (yp.reshape(nb, t, d), ys.reshape(db, ts, d),
            heads_t(kt), heads_t(vt), lft.transpose(0, 2, 1)[None],
            zc.reshape(nb, t, D_CONV)[:, t - (CONV_W - 1):, :][None],
            heads(ks_f, db, ts), heads(vs_f, db, ts), lfs.transpose(1, 2, 0)[None],
            zcs.reshape(db, ts, D_CONV)[:, ts - (CONV_W - 1):, :][None])
```

```python
import functools

import jax
import jax.numpy as jnp
import numpy as np
from jax import lax
from jax.experimental import pallas as pl
from jax.experimental.pallas import tpu as pltpu

F32 = jnp.float32
BF16 = jnp.bfloat16

D_MODEL = 2048
N_HEADS = 16
HEAD_DIM = 64
D_ATTN = N_HEADS * HEAD_DIM
D_CONV = 1024
CONV_W = 3
D_FF = 5632
RMS_EPS = 1e-6
ATTN_SCALE = HEAD_DIM ** -0.5
LOG2E = 1.4426950408889634
PAGE_SIZE = 128

LANES = 128
N_HP = D_ATTN // LANES
N_CHUNK = 8
CHUNK_W = 6 * (D_ATTN // N_CHUNK) + 2 * (D_MODEL // N_CHUNK)
VMEM_LIMIT = 56 * 1024 * 1024

TM_IN_PROJ = 1024
TM_MERGE = 512
TM_FFN = 512
TF_FFN = 512
TF_FFN_CAST = 256
T_ATTN = 512


def _params(n_axes):
    return pltpu.CompilerParams(
        dimension_semantics=("arbitrary",) * n_axes, vmem_limit_bytes=VMEM_LIMIT)


def _rms(x, g):
    return x * lax.rsqrt(jnp.mean(x * x, axis=-1, keepdims=True) + RMS_EPS) * g


def _repeat_rows(m, rep):
    n = m.shape[0]
    assert rep & (rep - 1) == 0
    src = lax.broadcasted_iota(jnp.int32, (n * rep, n), 0) >> (rep.bit_length() - 1)
    col = lax.broadcasted_iota(jnp.int32, (n * rep, n), 1)
    onehot = jnp.where(src == col, 1.0, 0.0).astype(BF16)
    out = None
    for _ in range(3):
        piece = m.astype(BF16)
        part = jnp.dot(onehot, piece, preferred_element_type=F32)
        out = part if out is None else out + part
        m = m - piece.astype(F32)
    return out


def _mod(ref, rep=1):
    if len(ref.shape) == 3:
        return ref[0]
    return ref[...] if rep == 1 else _repeat_rows(ref[...], rep)


def _log_sigmoid(z):
    return jnp.minimum(z, 0.0) - jnp.log1p(jnp.exp(-jnp.abs(z)))


def _adaln_kernel(c_ref, w_ref, b_ref, o_ref):
    @pl.when(pl.program_id(0) == 0)
    def _():
        o_ref[...] = jnp.broadcast_to(b_ref[...], o_ref.shape)

    c = c_ref[...]
    s = (c * jax.nn.sigmoid(c)).astype(BF16)
    o_ref[...] += jnp.dot(s, w_ref[...].astype(BF16), preferred_element_type=F32)


def _adaln(c, w_ada, b_ada):
    m, d = c.shape
    n = w_ada.shape[1]
    tk = 2 * LANES
    return pl.pallas_call(
        _adaln_kernel,
        grid=(d // tk,),
        in_specs=[pl.BlockSpec((m, tk), lambda k: (0, k)),
                  pl.BlockSpec((tk, n), lambda k: (k, 0)),
                  pl.BlockSpec((1, n), lambda k: (0, 0))],
        out_specs=pl.BlockSpec((m, n), lambda k: (0, 0)),
        out_shape=jax.ShapeDtypeStruct((m, n), F32),
        compiler_params=_params(1),
        name="adaln",
    )(c, w_ada, b_ada.reshape(1, n))


def _in_proj_kernel(*refs, tm, blocks_per_batch, per_row, rep, q_scale):
    if per_row:
        (x_ref, g_ref, sh_ref, sc_ref, w_ref, wf_ref, bf_ref, wc_ref, s0_ref, s1_ref,
         q_ref, kt_ref, vt_ref, kb_ref, vtb_ref, zc_ref, yb_ref, sga_ref, sgb_ref,
         lft_ref, h_scr) = refs
    else:
        (x_ref, g_ref, sh_ref, sc_ref, w_ref, wf_ref, bf_ref, wc_ref,
         q_ref, kt_ref, vt_ref, kb_ref, vtb_ref, zc_ref, yb_ref, sga_ref, sgb_ref,
         lft_ref, h_scr, halo_scr) = refs
    i = pl.program_id(0)
    j = pl.program_id(1)

    @pl.when(j == 0)
    def _():
        h = _rms(x_ref[...], g_ref[...]) * (1.0 + _mod(sc_ref, rep)) + _mod(sh_ref, rep)
        hb = h.astype(BF16)
        h_scr[...] = hb
        f = jnp.dot(hb, wf_ref[...], preferred_element_type=F32)
        lf = _log_sigmoid(f + bf_ref[...])
        lft_ref[0] = lf.T[0:N_HEADS, :]

    res = jnp.dot(h_scr[...], w_ref[...], preferred_element_type=F32)
    cw = D_ATTN // N_CHUNK
    gw = D_MODEL // N_CHUNK
    u = res[:, 0:cw]
    cg = res[:, cw:2 * cw]
    bg = res[:, 2 * cw:3 * cw]
    ga = res[:, 3 * cw:3 * cw + gw]
    gb = res[:, 3 * cw + gw:3 * cw + 2 * gw]
    o = 3 * cw + 2 * gw
    q = res[:, o:o + cw]
    k = res[:, o + cw:o + 2 * cw]
    v = res[:, o + 2 * cw:o + 3 * cw]

    q_ref[0] = (q * q_scale).astype(BF16)
    kt_ref[0] = k.T
    vt = v.T
    vt_ref[0] = vt
    vtb_ref[0] = vt.astype(BF16)
    kb_ref[0] = k.astype(BF16)
    sga_ref[...] = jax.nn.sigmoid(ga).astype(BF16)
    sgb_ref[...] = jax.nn.sigmoid(gb).astype(BF16)

    zc = cg * u
    zc_ref[...] = zc
    row = lax.broadcasted_iota(jnp.int32, zc.shape, 0)
    r1 = pltpu.roll(zc, 1, 0)
    r2 = pltpu.roll(zc, 2, 0)
    if per_row:
        t = row & 3
        s0 = s0_ref[...]
        s1 = s1_ref[...]
        p1 = jnp.where(t >= 1, r1, s1)
        p2 = jnp.where(t >= 2, r2, jnp.where(t == 1, s1, s0))
    else:
        first = (i % blocks_per_batch) == 0
        halo = halo_scr[j]
        h6 = jnp.where(first, 0.0, halo[6:7, :])
        h7 = jnp.where(first, 0.0, halo[7:8, :])
        p1 = jnp.where(row == 0, h7, r1)
        p2 = jnp.where(row == 0, h6, jnp.where(row == 1, h7, r2))
        halo_scr[j] = zc[tm - 8:tm, :]
    wc = wc_ref[...]
    y = p2 * wc[0:1, :]
    y = y + p1 * wc[1:2, :]
    y = y + zc * wc[2:3, :]
    yb_ref[...] = (bg * y).astype(BF16)


def _in_proj(x, g_pre, shift, scale, w_main, w_f, b_f, w_conv, state, *, tm, rows_per_batch):
    n = x.shape[0]
    per_row = state is not None
    nblk = n // tm
    cw = D_ATTN // N_CHUNK
    gw = D_MODEL // N_CHUNK
    if per_row:
        bpb = 1
        mod_spec = lambda c: pl.BlockSpec((tm // rows_per_batch, D_MODEL), lambda i, j: (i, c))
    else:
        bpb = rows_per_batch // tm
        mod_spec = lambda c: pl.BlockSpec((1, 1, D_MODEL), lambda i, j: (i // bpb, 0, c))
    in_specs = [
        pl.BlockSpec((tm, D_MODEL), lambda i, j: (i, 0)),
        pl.BlockSpec((1, D_MODEL), lambda i, j: (0, 0)),
        mod_spec(0), mod_spec(1),
        pl.BlockSpec((D_MODEL, CHUNK_W), lambda i, j: (0, j)),
        pl.BlockSpec((D_MODEL, LANES), lambda i, j: (0, 0)),
        pl.BlockSpec((1, LANES), lambda i, j: (0, 0)),
        pl.BlockSpec((CONV_W, cw), lambda i, j: (0, j)),
    ]
    args = [x, g_pre, shift, scale, w_main, w_f, b_f, w_conv]
    scratch = [pltpu.VMEM((tm, D_MODEL), BF16)]
    if per_row:
        in_specs += [pl.BlockSpec((tm, cw), lambda i, j: (i, j))] * 2
        args += [state[0], state[1]]
    else:
        scratch.append(pltpu.VMEM((N_CHUNK, 8, cw), F32))
    hp_spec = pl.BlockSpec((1, tm, LANES), lambda i, j: (j, i, 0))
    col_spec = lambda w: pl.BlockSpec((tm, w), lambda i, j: (i, j))
    nbat = nblk // bpb
    t_spec = lambda rows: pl.BlockSpec(
        (1, rows, tm), lambda i, j: (i // bpb, j if rows == LANES else 0, i % bpb))
    t_shape = lambda rows, dt=F32: jax.ShapeDtypeStruct((nbat, rows, bpb * tm), dt)
    out_specs = [hp_spec, t_spec(LANES), t_spec(LANES), hp_spec, t_spec(LANES),
                 col_spec(cw), col_spec(cw), col_spec(gw), col_spec(gw),
                 t_spec(N_HEADS)]
    hp_shape = jax.ShapeDtypeStruct((N_HP, n, LANES), BF16)
    out_shape = [hp_shape, t_shape(D_ATTN), t_shape(D_ATTN), hp_shape, t_shape(D_ATTN, BF16),
                 jax.ShapeDtypeStruct((n, D_CONV), F32), jax.ShapeDtypeStruct((n, D_CONV), BF16),
                 jax.ShapeDtypeStruct((n, D_MODEL), BF16), jax.ShapeDtypeStruct((n, D_MODEL), BF16),
                 t_shape(N_HEADS)]
    return pl.pallas_call(
        functools.partial(_in_proj_kernel, tm=tm, blocks_per_batch=bpb, per_row=per_row,
                          rep=rows_per_batch if per_row else 1,
                          q_scale=ATTN_SCALE if per_row else ATTN_SCALE * LOG2E),
        grid=(nblk, N_CHUNK),
        in_specs=in_specs, out_specs=out_specs, out_shape=out_shape,
        scratch_shapes=scratch,
        compiler_params=_params(2),
        name="in_proj_sample" if per_row else "in_proj_prompt",
    )(*args)


def _lane_cumsum(x):
    n = x.shape[-1]
    lane = lax.broadcasted_iota(jnp.int32, x.shape, x.ndim - 1)
    s = 1
    while s < n:
        x = x + jnp.where(lane >= s, pltpu.roll(x, s, x.ndim - 1), 0.0)
        s *= 2
    return x


def _cumsum_kernel(lft_ref, cumt_ref):
    cumt_ref[0] = _lane_cumsum(lft_ref[0]) * LOG2E


def _cumsum(lft, n_batch, t):
    spec = pl.BlockSpec((1, N_HEADS, t), lambda b: (b, 0, 0))
    return pl.pallas_call(
        _cumsum_kernel,
        grid=(n_batch,),
        in_specs=[spec], out_specs=spec,
        out_shape=jax.ShapeDtypeStruct((n_batch, N_HEADS, t), F32),
        compiler_params=_params(1),
        name="cumsum",
    )(lft)


ACC_ROWS = HEAD_DIM + 16


def _prompt_attn_kernel(qi_ref, kj_ref, q_ref, k_ref, vt_ref, cq_ref, ck_ref, o_ref,
                        acc_scr, m_scr, lg_scr, st_scr, mask_scr, *, tq, tk):
    t = pl.program_id(1)
    qi = qi_ref[t]
    kj = kj_ref[t]
    assert tq == tk

    @pl.when(kj == 0)
    def _():
        acc_scr[...] = jnp.zeros_like(acc_scr)
        m_scr[...] = jnp.full_like(m_scr, -jnp.inf)

    lane = lax.broadcasted_iota(jnp.int32, (1, LANES), 1)
    lo_half = lane < HEAD_DIM
    ones_rows = jnp.ones((ACC_ROWS - HEAD_DIM, tk), BF16)
    row8 = lax.broadcasted_iota(jnp.int32, (8, tq), 0)

    def logits(head, slot, masked):
        hp, e = divmod(head, 2)
        sel = lo_half if e == 0 else jnp.logical_not(lo_half)
        k2 = k_ref[hp]
        km = jnp.where(sel, k2, jnp.zeros_like(k2))
        st = lax.dot_general(km, q_ref[hp], (((1,), (1,)), ((), ())),
                             preferred_element_type=F32)
        ck_col = jnp.broadcast_to(ck_ref[0, head:head + 1, :], (LANES, tk)).T
        lg = st - pltpu.repeat(ck_col, tq // LANES, axis=1)
        if masked:
            lg = lg + mask_scr[...]
        cq = cq_ref[0, head:head + 1, :]
        m_prev = m_scr[hp, e:e + 1, :]
        m_new = jnp.maximum(m_prev, jnp.max(lg, axis=0, keepdims=True) + cq)
        lg_scr[slot] = lg
        st_scr[slot] = jnp.where(row8 == 0, cq - m_new, jnp.exp2(m_prev - m_new))
        m_scr[hp, e:e + 1, :] = m_new

    def update(head, slot, last):
        hp, e = divmod(head, 2)
        rows = slice(e * ACC_ROWS, (e + 1) * ACC_ROWS)
        stat = st_scr[slot]
        p = jnp.exp2(lg_scr[slot] + stat[0:1, :]).astype(BF16)
        vaug = jnp.concatenate(
            [vt_ref[0, head * HEAD_DIM:(head + 1) * HEAD_DIM, :], ones_rows], axis=0)
        acc = acc_scr[hp, rows, :] * stat[1:2, :] + jnp.dot(
            vaug, p, preferred_element_type=F32)
        if last:
            ot = acc[0:HEAD_DIM, :] * (1.0 / acc[HEAD_DIM:HEAD_DIM + 1, :])
            if e == 0:
                acc_scr[hp, 0:HEAD_DIM, :] = ot
            else:
                o_ref[hp] = jnp.concatenate(
                    [acc_scr[hp, 0:HEAD_DIM, :], ot], axis=0).T.astype(BF16)
        else:
            acc_scr[hp, rows, :] = acc

    def all_pairs(diagonal):
        logits(0, 0, diagonal)
        for head in range(N_HEADS):
            if head + 1 < N_HEADS:
                logits(head + 1, (head + 1) % 2, diagonal)
            update(head, head % 2, diagonal)

    @pl.when(kj < qi)
    def _():
        all_pairs(False)

    @pl.when(kj == qi)
    def _():
        key = lax.broadcasted_iota(jnp.int32, (tk, tq), 0)
        qry = lax.broadcasted_iota(jnp.int32, (tk, tq), 1)
        mask_scr[...] = jnp.where(key <= qry, 0.0, -jnp.inf)
        all_pairs(True)


def _prompt_attn(q_hp, k_hp, vt_b, cumt, n_batch, t, tq, tk):
    nq = t // tq
    nk = t // tk
    pairs = [(a, b) for a in range(nq) for b in range((a + 1) * tq // tk)]
    qi_tab = jnp.asarray([p[0] for p in pairs], jnp.int32)
    kj_tab = jnp.asarray([p[1] for p in pairs], jnp.int32)
    qblk = (N_HP, tq, LANES)
    grid_spec = pltpu.PrefetchScalarGridSpec(
        num_scalar_prefetch=2,
        grid=(n_batch, len(pairs)),
        in_specs=[
            pl.BlockSpec(qblk, lambda b, s, qi, kj: (0, b * nq + qi[s], 0)),
            pl.BlockSpec((N_HP, tk, LANES), lambda b, s, qi, kj: (0, b * nk + kj[s], 0)),
            pl.BlockSpec((1, D_ATTN, tk), lambda b, s, qi, kj: (b, 0, kj[s])),
            pl.BlockSpec((1, N_HEADS, tq), lambda b, s, qi, kj: (b, 0, qi[s])),
            pl.BlockSpec((1, N_HEADS, tk), lambda b, s, qi, kj: (b, 0, kj[s])),
        ],
        out_specs=pl.BlockSpec(qblk, lambda b, s, qi, kj: (0, b * nq + qi[s], 0)),
        scratch_shapes=[pltpu.VMEM((N_HP, 2 * ACC_ROWS, tq), F32),
                        pltpu.VMEM((N_HP, 8, tq), F32),
                        pltpu.VMEM((2, tk, tq), F32),
                        pltpu.VMEM((2, 8, tq), F32),
                        pltpu.VMEM((tk, tq), F32)],
    )
    return pl.pallas_call(
        functools.partial(_prompt_attn_kernel, tq=tq, tk=tk),
        grid_spec=grid_spec,
        out_shape=jax.ShapeDtypeStruct((N_HP, n_batch * t, LANES), BF16),
        compiler_params=_params(2),
        name="prompt_attn",
    )(qi_tab, kj_tab, q_hp, k_hp, vt_b, cumt, cumt)


def _sample_attn_kernel(*refs, n_pages, n_new):
    pt_ref = refs[0]
    q_ref, kn_ref, vn_ref, lfn_ref = refs[1:5]
    k_refs = refs[5:5 + n_pages]
    v_refs = refs[5 + n_pages:5 + 2 * n_pages]
    lf_refs = refs[5 + 2 * n_pages:5 + 3 * n_pages]
    o_ref = refs[5 + 3 * n_pages]
    del pt_ref
    nr = n_new * N_HEADS
    nt = ((1,), (1,)), ((), ())
    page_t = lambda ref: ref[0].reshape(D_ATTN, PAGE_SIZE).astype(BF16)

    q = q_ref[0]
    hrow = lax.broadcasted_iota(jnp.int32, (N_HEADS, D_ATTN), 0)
    hlane = lax.broadcasted_iota(jnp.int32, (N_HEADS, D_ATTN), 1) // HEAD_DIM
    hmask = hrow == hlane
    qbd = jnp.concatenate(
        [jnp.where(hmask, jnp.broadcast_to(q[qq:qq + 1, :], (N_HEADS, D_ATTN)), 0.0)
         for qq in range(n_new)], axis=0).astype(BF16)

    s_parts = []
    lf_parts = []
    for c in range(n_pages):
        s_parts.append(jnp.dot(qbd, page_t(k_refs[c]), preferred_element_type=F32))
        lf_parts.append(lf_refs[c][0])
    s_past = jnp.concatenate(s_parts, axis=1)
    ck_past = _lane_cumsum(jnp.concatenate(lf_parts, axis=1))
    n_past = ck_past.shape[1]
    c_total = ck_past[:, n_past - 1:n_past]

    kn = jnp.concatenate(
        [kn_ref[0], jnp.zeros((PAGE_SIZE - 8, D_ATTN), F32)], axis=0).astype(BF16)
    vn = jnp.concatenate(
        [vn_ref[0], jnp.zeros((PAGE_SIZE - 8, D_ATTN), F32)], axis=0).astype(BF16)
    s_new = lax.dot_general(qbd, kn, nt, preferred_element_type=F32)
    ck_new = c_total + _lane_cumsum(lfn_ref[0])

    tile = lambda a: jnp.concatenate([a] * n_new, axis=0)
    ck_past_r = tile(ck_past)
    ck_new_r = tile(ck_new)
    qq_row = lax.broadcasted_iota(jnp.int32, (nr, PAGE_SIZE), 0) // N_HEADS
    lane = lax.broadcasted_iota(jnp.int32, (nr, PAGE_SIZE), 1)
    cq = jnp.sum(jnp.where(lane == qq_row, ck_new_r, 0.0), axis=1, keepdims=True)

    lg_past = s_past + (cq - ck_past_r)
    lg_new = jnp.where(lane <= qq_row, s_new + (cq - ck_new_r), -jnp.inf)
    m = jnp.maximum(jnp.max(lg_past, axis=1, keepdims=True),
                    jnp.max(lg_new, axis=1, keepdims=True))
    p_past = jnp.exp(lg_past - m)
    p_new = jnp.exp(lg_new - m)
    denom = jnp.sum(p_past, axis=1, keepdims=True) + jnp.sum(p_new, axis=1, keepdims=True)
    acc = jnp.dot(p_new.astype(BF16), vn, preferred_element_type=F32)
    pb = p_past.astype(BF16)
    for c in range(n_pages):
        acc = acc + lax.dot_general(pb[:, c * PAGE_SIZE:(c + 1) * PAGE_SIZE], page_t(v_refs[c]),
                                    nt, preferred_element_type=F32)
    acc = acc / denom
    orow = lax.broadcasted_iota(jnp.int32, (8, D_ATTN), 0)
    out = jnp.zeros((8, D_ATTN), F32)
    for qq in range(n_new):
        blk = jnp.where(hmask, acc[qq * N_HEADS:(qq + 1) * N_HEADS, :], 0.0)
        out = jnp.where(orow == qq, jnp.sum(blk, axis=0, keepdims=True), out)
    o_ref[0] = out


def _sample_attn(page_table, q8, kn8, vn8, lfn_t, cache_kt, cache_vt, cache_lft, n_new):
    n_batch, n_pages = page_table.shape
    row_spec = lambda w: pl.BlockSpec((1, 8, w), lambda b, pt: (b, 0, 0))
    kv_page = lambda c: pl.BlockSpec((1, N_HEADS, HEAD_DIM, PAGE_SIZE),
                                     lambda b, pt: (pt[b * n_pages + c], 0, 0, 0))
    lf_page = lambda c: pl.BlockSpec((1, N_HEADS, PAGE_SIZE),
                                     lambda b, pt: (pt[b * n_pages + c], 0, 0))
    in_specs = [row_spec(D_ATTN), row_spec(D_ATTN), row_spec(D_ATTN),
                pl.BlockSpec((1, N_HEADS, LANES), lambda b, pt: (b, 0, 0))]
    in_specs += [kv_page(c) for c in range(n_pages)]
    in_specs += [kv_page(c) for c in range(n_pages)]
    in_specs += [lf_page(c) for c in range(n_pages)]
    grid_spec = pltpu.PrefetchScalarGridSpec(
        num_scalar_prefetch=1,
        grid=(n_batch,),
        in_specs=in_specs,
        out_specs=row_spec(D_ATTN),
    )
    return pl.pallas_call(
        functools.partial(_sample_attn_kernel, n_pages=n_pages, n_new=n_new),
        grid_spec=grid_spec,
        out_shape=jax.ShapeDtypeStruct((n_batch, 8, D_ATTN), F32),
        compiler_params=_params(1),
        name="sample_attn",
    )(page_table.reshape(-1), q8, kn8, vn8, lfn_t,
      *([cache_kt] * n_pages), *([cache_vt] * n_pages), *([cache_lft] * n_pages))


def _merge_kernel(o_ref, yb_ref, sga_ref, sgb_ref, x_ref, gt_ref, sh_ref, sc_ref,
                  gp1_ref, gp2_ref, woa_ref, wob_ref, wo_ref, x1_ref, h2_ref, *, rep):
    tm = x_ref.shape[0]
    gate, shift, scale = _mod(gt_ref, rep), _mod(sh_ref, rep), _mod(sc_ref, rep)
    n_split = 2 if tm % 512 == 0 else 1
    for r in range(n_split):
        rows = slice(r * tm // n_split, (r + 1) * tm // n_split)
        sub = lambda v: v if v.shape[0] == 1 else v[rows, :]
        o = jnp.concatenate([o_ref[hp, rows, :] for hp in range(N_HP)], axis=1)
        a = jnp.dot(o, woa_ref[...], preferred_element_type=F32)
        b = jnp.dot(yb_ref[rows, :], wob_ref[...], preferred_element_type=F32)
        mm = (sga_ref[rows, :].astype(F32) * a + sgb_ref[rows, :].astype(F32) * b).astype(BF16)
        m = jnp.dot(mm, wo_ref[...], preferred_element_type=F32)
        x1 = x_ref[rows, :] + sub(gate) * _rms(m, gp1_ref[...])
        x1_ref[rows, :] = x1
        h2 = _rms(x1, gp2_ref[...]) * (1.0 + sub(scale)) + sub(shift)
        h2_ref[rows, :] = h2.astype(BF16)


def _merge(o_hp, yb, sga, sgb, x, mods, g_post1, g_pre2, w_oa, w_ob, w_o, *, tm,
           rows_per_batch, per_row):
    n = x.shape[0]
    if per_row:
        mod_spec = lambda c: pl.BlockSpec((tm // rows_per_batch, D_MODEL), lambda i: (i, c))
    else:
        bpb = rows_per_batch // tm
        mod_spec = lambda c: pl.BlockSpec((1, 1, D_MODEL), lambda i: (i // bpb, 0, c))
    row_spec = lambda w: pl.BlockSpec((tm, w), lambda i: (i, 0))
    const = lambda shape: pl.BlockSpec(shape, lambda i: (0,) * len(shape),
                                       pipeline_mode=pl.Buffered(1))
    return pl.pallas_call(
        functools.partial(_merge_kernel, rep=rows_per_batch if per_row else 1),
        grid=(n // tm,),
        in_specs=[pl.BlockSpec((N_HP, tm, LANES), lambda i: (0, i, 0)),
                  row_spec(D_CONV), row_spec(D_MODEL), row_spec(D_MODEL), row_spec(D_MODEL),
                  mod_spec(2), mod_spec(3), mod_spec(4),
                  const((1, D_MODEL)), const((1, D_MODEL)),
                  const((D_ATTN, D_MODEL)), const((D_CONV, D_MODEL)),
                  const((D_MODEL, D_MODEL))],
        out_specs=[row_spec(D_MODEL), row_spec(D_MODEL)],
        out_shape=[jax.ShapeDtypeStruct((n, D_MODEL), F32),
                   jax.ShapeDtypeStruct((n, D_MODEL), BF16)],
        compiler_params=_params(1),
        name="merge_sample" if per_row else "merge_prompt",
    )(o_hp, yb, sga, sgb, x, mods, mods, mods, g_post1, g_pre2, w_oa, w_ob, w_o)


def _ffn_kernel(h_ref, wg_ref, wu_ref, wout_ref, x1_ref, gt_ref, gp_ref, y_ref, *rest,
                rep, emit_bf16):
    acc_scr = rest[-1]
    j = pl.program_id(1)

    @pl.when(j == 0)
    def _():
        acc_scr[...] = jnp.zeros_like(acc_scr)

    wg, wu, wout = wg_ref[...], wu_ref[...], wout_ref[...]
    if emit_bf16:
        wg, wu, wout = wg.astype(BF16), wu.astype(BF16), wout.astype(BF16)
        rest[0][...], rest[1][...], rest[2][...] = wg, wu, wout
    h = h_ref[...]
    g = jnp.dot(h, wg, preferred_element_type=F32)
    u = jnp.dot(h, wu, preferred_element_type=F32)
    act = (g * jax.nn.sigmoid(g) * u).astype(BF16)
    acc_scr[...] += jnp.dot(act, wout, preferred_element_type=F32)

    @pl.when(j == pl.num_programs(1) - 1)
    def _():
        y_ref[...] = x1_ref[...] + _mod(gt_ref, rep) * _rms(acc_scr[...], gp_ref[...])


def _ffn(h2, x1, mods, g_post2, w_gate, w_up, w_out, *, tm, tf, rows_per_batch, per_row,
         emit_bf16=False):
    n = x1.shape[0]
    nf = D_FF // tf
    if per_row:
        mod_spec = pl.BlockSpec((tm // rows_per_batch, D_MODEL), lambda i, j: (i, 5))
    else:
        bpb = rows_per_batch // tm
        mod_spec = pl.BlockSpec((1, 1, D_MODEL), lambda i, j: (i // bpb, 0, 5))
    row_spec = pl.BlockSpec((tm, D_MODEL), lambda i, j: (i, 0))
    col0 = [c // tf for _, c in (w_gate, w_up)]
    in_col_spec = lambda c0: pl.BlockSpec((D_MODEL, tf), lambda i, j: (0, j + c0))
    col_spec = pl.BlockSpec((D_MODEL, tf), lambda i, j: (0, j))
    out_w_spec = pl.BlockSpec((tf, D_MODEL), lambda i, j: (j, 0))
    out_specs = [row_spec]
    out_shape = [jax.ShapeDtypeStruct((n, D_MODEL), F32)]
    if emit_bf16:
        assert n == tm
        out_specs += [col_spec, col_spec, out_w_spec]
        out_shape += [jax.ShapeDtypeStruct((D_MODEL, D_FF), BF16)] * 2
        out_shape += [jax.ShapeDtypeStruct((D_FF, D_MODEL), BF16)]
    outs = pl.pallas_call(
        functools.partial(_ffn_kernel, rep=rows_per_batch if per_row else 1,
                          emit_bf16=emit_bf16),
        grid=(n // tm, nf),
        in_specs=[row_spec, in_col_spec(col0[0]), in_col_spec(col0[1]), out_w_spec,
                  row_spec, mod_spec,
                  pl.BlockSpec((1, D_MODEL), lambda i, j: (0, 0))],
        out_specs=out_specs, out_shape=out_shape,
        scratch_shapes=[pltpu.VMEM((tm, D_MODEL), F32)],
        compiler_params=_params(2),
        name="ffn_sample" if per_row else "ffn_prompt",
    )(h2, w_gate[0], w_up[0], w_out, x1, mods, g_post2)
    return outs if emit_bf16 else outs[0]


def _w_chunk_kernel(tab_ref, *refs):
    del tab_ref
    f_ref, o_ref, of_ref = refs[-3:]
    for pos, w_ref in enumerate(refs[:-3]):
        o_ref[:, pos * LANES:(pos + 1) * LANES] = w_ref[...].T.astype(BF16)

    @pl.when(pl.program_id(0) == 0)
    def _():
        ft = f_ref[...].T
        lane = lax.broadcasted_iota(jnp.int32, ft.shape, 1)
        of_ref[...] = jnp.where(lane < N_HEADS, ft, 0.0).astype(BF16)


def _chunked_in_weights(w_in):
    d = w_in.shape[0]
    offs = np.cumsum([0, D_ATTN, D_ATTN, D_ATTN, N_HEADS, D_CONV, D_CONV, D_CONV,
                      D_MODEL, D_MODEL])
    q, k, v, f, u, bg, cg, ga, gb = [int(o) for o in offs[:9]]
    cw = D_ATTN // N_CHUNK
    gw = D_MODEL // N_CHUNK
    starts = []
    for j in range(N_CHUNK):
        starts += [u + cw * j, cg + cw * j, bg + cw * j]
        starts += [ga + gw * j + s for s in range(0, gw, LANES)]
        starts += [gb + gw * j + s for s in range(0, gw, LANES)]
        starts += [q + cw * j, k + cw * j, v + cw * j]
    n_slab = CHUNK_W // LANES
    slab = lambda pos: pl.BlockSpec((pl.Element(LANES), pl.Element(d)),
                                    lambda j, tab: (pl.multiple_of(tab[j * n_slab + pos], 8), 0))
    f_slab = pl.BlockSpec((pl.Element(LANES), pl.Element(d)), lambda j, tab: (f, 0))
    return pl.pallas_call(
        _w_chunk_kernel,
        grid_spec=pltpu.PrefetchScalarGridSpec(
            num_scalar_prefetch=1, grid=(N_CHUNK,),
            in_specs=[slab(pos) for pos in range(n_slab)] + [f_slab],
            out_specs=[pl.BlockSpec((d, CHUNK_W), lambda j, tab: (0, j)),
                       pl.BlockSpec((d, LANES), lambda j, tab: (0, 0))]),
        out_shape=[jax.ShapeDtypeStruct((d, N_CHUNK * CHUNK_W), BF16),
                   jax.ShapeDtypeStruct((d, LANES), BF16)],
        compiler_params=_params(1),
        name="w_chunk",
    )(jnp.asarray(starts, jnp.int32), *([w_in.T] * (n_slab + 1)))


def kernel(x_prompt, x_sample, cache_k, cache_v, cache_logf, state_conv, page_table,
           c_prompt, c_sample, w_ada, b_ada, g_pre1, w_in, b_f, w_conv, w_oa, w_ob, w_o,
           g_post1, g_pre2, w_ffn_in, w_ffn_out, g_post2):
    depth = w_in.shape[0]
    assert depth == 1
    nb, t, d = x_prompt.shape
    db, ts, _ = x_sample.shape
    l = 0

    w_main, w_f = _chunked_in_weights(w_in[l])
    b_f_pad = jnp.pad(b_f[l], (0, LANES - N_HEADS)).reshape(1, LANES)
    w_oa_b = w_oa[l].astype(BF16)
    w_ob_b = w_ob[l].astype(BF16)
    w_o_b = w_o[l].astype(BF16)
    row2 = lambda a: a.reshape(1, -1)

    mods = _adaln(jnp.concatenate([c_prompt, c_sample], axis=0), w_ada[l], b_ada[l])
    mods_p = mods[:nb].reshape(nb, 1, 6 * d)
    mods_s = mods[nb:]

    xp = x_prompt.reshape(nb * t, d)
    (q_hp, kt, vt, k_hp, vt_b, zc, yb, sga, sgb, lft) = _in_proj(
        xp, row2(g_pre1[l]), mods_p, mods_p, w_main, w_f, b_f_pad, w_conv[l], None,
        tm=TM_IN_PROJ, rows_per_batch=t)
    cumt = _cumsum(lft, nb, t)
    o_hp = _prompt_attn(q_hp, k_hp, vt_b, cumt, nb, t, T_ATTN, T_ATTN)
    x1, h2 = _merge(o_hp, yb, sga, sgb, xp, mods_p, row2(g_post1[l]), row2(g_pre2[l]),
                    w_oa_b, w_ob_b, w_o_b, tm=TM_MERGE, rows_per_batch=t, per_row=False)

    ns = db * ts
    xs = x_sample.reshape(ns, d)
    st = state_conv[l].astype(F32)
    state = (jnp.repeat(st[:, 0, :], ts, axis=0), jnp.repeat(st[:, 1, :], ts, axis=0))
    (qs_hp, kst, vst, _, _, zcs, ybs, sgas, sgbs, lfst) = _in_proj(
        xs, row2(g_pre1[l]), mods_s, mods_s, w_main, w_f, b_f_pad, w_conv[l], state,
        tm=ns, rows_per_batch=ts)
    ks_f = kst[0].T
    vs_f = vst[0].T
    lfs = lfst[0].reshape(N_HEADS, db, ts)
    pad8 = lambda a: jnp.pad(a.reshape(db, ts, -1), ((0, 0), (0, 8 - ts), (0, 0)))
    qs = qs_hp.transpose(1, 0, 2).reshape(ns, D_ATTN).astype(F32)
    lfn_t = jnp.pad(lfs.transpose(1, 0, 2), ((0, 0), (0, 0), (0, LANES - ts)))
    os8 = _sample_attn(page_table, pad8(qs), pad8(ks_f), pad8(vs_f), lfn_t,
                       cache_k[l].transpose(0, 2, 3, 1), cache_v[l].transpose(0, 2, 3, 1),
                       cache_logf[l].transpose(0, 2, 1), ts)
    os_hp = (os8[:, :ts, :].reshape(ns, N_HP, LANES).transpose(1, 0, 2).astype(BF16))
    x1s, h2s = _merge(os_hp, ybs, sgas, sgbs, xs, mods_s, row2(g_post1[l]), row2(g_pre2[l]),
                      w_oa_b, w_ob_b, w_o_b, tm=ns, rows_per_batch=ts, per_row=True)
    ys, w_gate_b, w_up_b, w_out_b = _ffn(
        h2s, x1s, mods_s, row2(g_post2[l]), (w_ffn_in[l], 0), (w_ffn_in[l], D_FF),
        w_ffn_out[l], tm=ns, tf=TF_FFN_CAST, rows_per_batch=ts, per_row=True, emit_bf16=True)
    yp = _ffn(h2, x1, mods_p, row2(g_post2[l]), (w_gate_b, 0), (w_up_b, 0), w_out_b,
              tm=TM_FFN, tf=TF_FFN, rows_per_batch=t, per_row=False)

    heads = lambda a, b_, t_: a.reshape(1, b_, t_, N_HEADS, HEAD_DIM)
    heads_t = lambda a: a.reshape(nb, N_HEADS, HEAD_DIM, t).transpose(0, 3, 1, 2)[None]
    return (yp.reshape(nb, t, d), ys.reshape(db, ts, d),
            heads_t(kt), heads_t(vt), lft.transpose(0, 2, 1)[None],
            zc.reshape(nb, t, D_CONV)[:, t - (CONV_W - 1):, :][None],
            heads(ks_f, db, ts), heads(vs_f, db, ts), lfs.transpose(1, 2, 0)[None],
            zcs.reshape(db, ts, D_CONV)[:, ts - (CONV_W - 1):, :][None])
```
